```python
import math
import jax, jax.numpy as jnp
from jax import lax
import numpy as np

D_MODEL = 1024
BATCH = 8
SEQ = 2048
DEPTH = 1

GRID_W = 64
CTX_LEN = 256

MLA_HEADS = 8
MLA_NOPE = 64
MLA_ROPE = 32
MLA_QK = MLA_NOPE + MLA_ROPE
MLA_V = 64
MLA_Q_LORA = 384
MLA_KV_LORA = 256
MLA_WIDTH = MLA_HEADS * MLA_V
ROPE_AXIS = MLA_ROPE // 2
ROPE_THETA = 10000.0
Q_BLOCK = 128

DN_HEADS = 8
DN_DK = 64
DN_DV = 64
DN_WIDTH_K = DN_HEADS * DN_DK
DN_WIDTH = DN_HEADS * DN_DV
CONV_W = 5
CHUNK = 64
N_DIR = 2

EPS = 1e-6

IN_SPLITS = (MLA_Q_LORA, MLA_KV_LORA, MLA_ROPE, MLA_WIDTH,
             DN_WIDTH_K, DN_WIDTH_K, DN_WIDTH, DN_WIDTH,
             N_DIR * DN_HEADS, N_DIR * DN_HEADS, 2 * D_MODEL)
IN_DIM = (MLA_Q_LORA + MLA_KV_LORA + MLA_ROPE + MLA_WIDTH + 2 * DN_WIDTH_K + 2 * DN_WIDTH
          + 2 * N_DIR * DN_HEADS + 2 * D_MODEL)

kernel_name = "hybrid_mla_gated_deltanet_ctx_prefix"


def rms_norm(x, w):
    x32 = x.astype(jnp.float32)
    y = x32 * lax.rsqrt(jnp.mean(x32 * x32, axis=-1, keepdims=True) + EPS)
    return y.astype(x.dtype) * w


def l2_normalize(x):
    x32 = x.astype(jnp.float32)
    return (x32 * lax.rsqrt(jnp.sum(x32 * x32, axis=-1, keepdims=True) + EPS)).astype(x.dtype)


def split_in(p):
    offsets = np.cumsum(np.array(IN_SPLITS))[:-1].tolist()
    return jnp.split(p, offsets, axis=-1)


def axial_rope_tables(n_tokens, dtype):
    rows = n_tokens // GRID_W
    row = jnp.repeat(jnp.arange(rows, dtype=jnp.float32), GRID_W)
    col = jnp.tile(jnp.arange(GRID_W, dtype=jnp.float32), rows)
    inv_freq = ROPE_THETA ** (-jnp.arange(0, ROPE_AXIS, 2, dtype=jnp.float32) / ROPE_AXIS)
    ang = jnp.concatenate([row[:, None] * inv_freq, col[:, None] * inv_freq], axis=-1)
    return jnp.cos(ang).astype(dtype), jnp.sin(ang).astype(dtype)


def apply_axial_rope(t, cos, sin):
    half = ROPE_AXIS // 2
    tr = t.reshape(t.shape[:-1] + (2, 2, half))
    t1, t2 = tr[..., 0, :], tr[..., 1, :]
    c = cos.reshape(-1, 2, half)
    s = sin.reshape(-1, 2, half)
    out = jnp.stack([t1 * c - t2 * s, t1 * s + t2 * c], axis=-2)
    return out.reshape(t.shape)


def rope_tail(t, cos, sin):
    return jnp.concatenate([t[..., :MLA_NOPE], apply_axial_rope(t[..., MLA_NOPE:], cos, sin)], axis=-1)


def mla_queries(cq, q_norm_w, w_uq, q_head_norm_w):
    B, T, _ = cq.shape
    q = (rms_norm(cq, q_norm_w) @ w_uq).reshape(B, T, MLA_HEADS, MLA_QK)
    return rms_norm(q, q_head_norm_w).transpose(0, 2, 1, 3)


def mla_keys_values(ckv, krope, kv_norm_w, w_ukv, k_head_norm_w):
    B, T, _ = ckv.shape
    kv = (rms_norm(ckv, kv_norm_w) @ w_ukv).reshape(B, T, MLA_HEADS, MLA_NOPE + MLA_V)
    k_nope, v = kv[..., :MLA_NOPE], kv[..., MLA_NOPE:]
    k_rope = jnp.broadcast_to(krope[:, :, None, :], (B, T, MLA_HEADS, MLA_ROPE))
    k = rms_norm(jnp.concatenate([k_nope, k_rope], axis=-1), k_head_norm_w)
    return k.transpose(0, 2, 1, 3), v.transpose(0, 2, 1, 3)


def block_softmax_attention(q, k, v):
    B, H, T, dq = q.shape
    nb = T // Q_BLOCK
    scale = MLA_QK ** -0.5
    qb = q.reshape(B, H, nb, Q_BLOCK, dq).transpose(2, 0, 1, 3, 4)

    def attend(q_blk):
        s = jnp.einsum('bhqd,bhkd->bhqk', q_blk, k).astype(jnp.float32) * scale
        p = jax.nn.softmax(s, axis=-1).astype(v.dtype)
        return jnp.einsum('bhqk,bhkd->bhqd', p, v)

    o = lax.map(attend, qb)
    return o.transpose(1, 2, 0, 3, 4).reshape(B, H, T, v.shape[-1])


def centred_depthwise_conv(x, w):
    y = lax.conv_general_dilated(x, w[:, None, :].astype(x.dtype), window_strides=(1,),
                                 padding=[(CONV_W // 2, CONV_W // 2)],
                                 dimension_numbers=('NWC', 'WIO', 'NWC'),
                                 feature_group_count=x.shape[-1])
    return jax.nn.silu(y)


def deltanet_inputs(dq, dk, dv, db, da, conv_w, a_log, dt_bias):
    B, T, _ = dq.shape
    qkv = centred_depthwise_conv(jnp.concatenate([dq, dk, dv], axis=-1), conv_w)
    q, k, v = jnp.split(qkv, [DN_WIDTH_K, 2 * DN_WIDTH_K], axis=-1)
    heads = lambda t, d: t.reshape(B, T, DN_HEADS, d).transpose(0, 2, 1, 3)
    q = l2_normalize(heads(q, DN_DK)) * (DN_DK ** -0.5)
    k = l2_normalize(heads(k, DN_DK))
    v = heads(v, DN_DV)
    db = db.reshape(B, T, N_DIR, DN_HEADS).astype(jnp.float32)
    da = da.reshape(B, T, N_DIR, DN_HEADS).astype(jnp.float32)
    beta = jax.nn.sigmoid(db).transpose(2, 0, 3, 1)
    g = -jnp.exp(a_log.astype(jnp.float32))[:, None, :, None] * \
        jax.nn.softplus(da + dt_bias.astype(jnp.float32)).transpose(2, 0, 3, 1)
    return q, k, v, beta, g


def chunk_gated_delta(q, k, v, beta, g, s0):
    out_dtype = v.dtype
    f32 = jnp.float32
    B, H, T, dk = q.shape
    dv = v.shape[-1]
    n = T // CHUNK
    q = q.astype(f32).reshape(B, H, n, CHUNK, dk)
    k = k.astype(f32).reshape(B, H, n, CHUNK, dk)
    v = v.astype(f32).reshape(B, H, n, CHUNK, dv)
    beta = beta.astype(f32).reshape(B, H, n, CHUNK)
    gc = jnp.cumsum(g.astype(f32).reshape(B, H, n, CHUNK), axis=-1)
    incl = jnp.tril(jnp.ones((CHUNK, CHUNK), dtype=bool))
    strict = jnp.tril(jnp.ones((CHUNK, CHUNK), dtype=bool), -1)
    diff = gc[..., :, None] - gc[..., None, :]
    decay = jnp.where(incl, jnp.exp(jnp.where(incl, diff, 0.0)), 0.0)
    kb = k * beta[..., None]
    lower = jnp.where(strict, jnp.einsum('bhnid,bhnjd->bhnij', kb, k) * decay, 0.0)
    a_mat = lower + jnp.eye(CHUNK, dtype=f32)
    rhs = jnp.concatenate([v * beta[..., None], kb * jnp.exp(gc)[..., None]], axis=-1)
    sol = lax.linalg.triangular_solve(a_mat, rhs, left_side=True, lower=True, unit_diagonal=True)
    u, w = sol[..., :dv], sol[..., dv:]
    qk = jnp.where(incl, jnp.einsum('bhnid,bhnjd->bhnij', q, k) * decay, 0.0)
    g_last = gc[..., -1]
    q_dec = q * jnp.exp(gc)[..., None]
    k_dec = k * jnp.exp(g_last[..., None] - gc)[..., None]
    xs = tuple(jnp.moveaxis(t, 2, 0) for t in (q_dec, qk, u, w, k_dec, g_last))

    def step(state, inp):
        qd, a, u_c, w_c, kd, gl = inp
        v_new = u_c - jnp.einsum('bhcd,bhde->bhce', w_c, state)
        o = jnp.einsum('bhcd,bhde->bhce', qd, state) + jnp.einsum('bhcj,bhje->bhce', a, v_new)
        state = state * jnp.exp(gl)[..., None, None] + jnp.einsum('bhcd,bhce->bhde', kd, v_new)
        return state, o

    s_final, o = lax.scan(step, s0, xs)
    o = jnp.moveaxis(o, 0, 2).reshape(B, H, T, dv).astype(out_dtype)
    return o, s_final


def bidirectional_gated_delta(ctx_in, lat_in):
    qc, kc, vc, bc, gc = ctx_in
    ql, kl, vl, bl, gl = lat_in
    B, H = qc.shape[0], qc.shape[1]
    o_ctx, o_lat = 0.0, 0.0
    for d in range(N_DIR):
        flip = (lambda t: jnp.flip(t, axis=2)) if d == 1 else (lambda t: t)
        s0 = jnp.zeros((B, H, DN_DK, DN_DV), jnp.float32)
        oc, s_ctx = chunk_gated_delta(flip(qc), flip(kc), flip(vc), flip(bc[d]), flip(gc[d]), s0)
        ol, _ = chunk_gated_delta(flip(ql), flip(kl), flip(vl), flip(bl[d]), flip(gl[d]), s_ctx)
        o_ctx = o_ctx + flip(oc)
        o_lat = o_lat + flip(ol)
    return o_ctx, o_lat


def merge_branches(o_mla, z_mla, o_dn, z_dn, gates, mla_w_o, dn_out_norm_w, dn_w_o, w_out):
    B, _, T, _ = o_mla.shape
    y_mla = (o_mla.transpose(0, 2, 1, 3).reshape(B, T, MLA_WIDTH) * jax.nn.silu(z_mla)) @ mla_w_o
    o_dn = rms_norm(o_dn.transpose(0, 2, 1, 3), dn_out_norm_w).reshape(B, T, DN_WIDTH)
    y_dn = (o_dn * jax.nn.silu(z_dn)) @ dn_w_o
    g_mla, g_dn = jnp.split(gates, 2, axis=-1)
    return (jax.nn.sigmoid(g_mla) * y_mla + jax.nn.sigmoid(g_dn) * y_dn) @ w_out


def hybrid_layer(x, ctx, c, c_ctx, rope_cos, rope_sin, w_mod, b_mod, norm_w, w_in,
                 mla_q_norm_w, mla_w_uq, mla_kv_norm_w, mla_w_ukv, mla_q_head_norm_w, mla_k_head_norm_w, mla_w_o,
                 dn_conv_w, dn_a_log, dn_dt_bias, dn_out_norm_w, dn_w_o, w_out, update_ctx):
    shift, scale, gate = jnp.split(jax.nn.silu(c) @ w_mod + b_mod, 3, axis=-1)
    shift_c, scale_c, gate_c = jnp.split(jax.nn.silu(c_ctx) @ w_mod + b_mod, 3, axis=-1)
    h = rms_norm(x, norm_w) * (1.0 + scale[:, None]) + shift[:, None]
    hc = rms_norm(ctx, norm_w) * (1.0 + scale_c) + shift_c
    cq_l, ckv_l, kr_l, zm_l, dq_l, dk_l, dv_l, zd_l, db_l, da_l, gates_l = split_in(h @ w_in)
    cq_c, ckv_c, kr_c, zm_c, dq_c, dk_c, dv_c, zd_c, db_c, da_c, gates_c = split_in(hc @ w_in)

    k_ctx, v_ctx = mla_keys_values(ckv_c, kr_c, mla_kv_norm_w, mla_w_ukv, mla_k_head_norm_w)
    k_lat, v_lat = mla_keys_values(ckv_l, kr_l, mla_kv_norm_w, mla_w_ukv, mla_k_head_norm_w)
    k_lat = rope_tail(k_lat, rope_cos, rope_sin)
    q_lat = rope_tail(mla_queries(cq_l, mla_q_norm_w, mla_w_uq, mla_q_head_norm_w), rope_cos, rope_sin)
    o_mla_lat = block_softmax_attention(q_lat, jnp.concatenate([k_ctx, k_lat], axis=2),
                                        jnp.concatenate([v_ctx, v_lat], axis=2))

    dn_ctx = deltanet_inputs(dq_c, dk_c, dv_c, db_c, da_c, dn_conv_w, dn_a_log, dn_dt_bias)
    dn_lat = deltanet_inputs(dq_l, dk_l, dv_l, db_l, da_l, dn_conv_w, dn_a_log, dn_dt_bias)
    o_dn_ctx, o_dn_lat = bidirectional_gated_delta(dn_ctx, dn_lat)

    y_lat = merge_branches(o_mla_lat, zm_l, o_dn_lat, zd_l, gates_l, mla_w_o, dn_out_norm_w, dn_w_o, w_out)
    x = x + gate[:, None] * y_lat
    if update_ctx:
        q_ctx = mla_queries(cq_c, mla_q_norm_w, mla_w_uq, mla_q_head_norm_w)
        o_mla_ctx = block_softmax_attention(q_ctx, k_ctx, v_ctx)
        y_ctx = merge_branches(o_mla_ctx, zm_c, o_dn_ctx, zd_c, gates_c, mla_w_o, dn_out_norm_w, dn_w_o, w_out)
        ctx = ctx + gate_c * y_ctx
    return x, ctx


def setup_inputs(seed: int = 0) -> dict:
    key = jax.random.key(seed)
    ks = jax.random.split(key, 21)
    f32 = jnp.float32
    L = DEPTH
    nrm = lambda k, shape, fan_in: jax.random.normal(k, shape, f32) * (fan_in ** -0.5)
    gain = lambda k, shape: 1.0 + 0.01 * jax.random.normal(k, shape, f32)
    dt = jnp.exp(jax.random.uniform(ks[17], (L, N_DIR, DN_HEADS), f32, minval=math.log(1e-3), maxval=math.log(1e-1)))
    return {
        "x": jax.random.normal(ks[0], (BATCH, SEQ, D_MODEL), f32),
        "c": jax.random.normal(ks[1], (BATCH, D_MODEL), f32),
        "ctx": jax.random.normal(ks[2], (BATCH, CTX_LEN, D_MODEL), f32),
        "c_ctx": jax.random.normal(ks[3], (D_MODEL,), f32),
        "w_mod": nrm(ks[4], (L, D_MODEL, 3 * D_MODEL), D_MODEL),
        "b_mod": 0.02 * jax.random.normal(ks[5], (L, 3 * D_MODEL), f32),
        "norm_w": gain(ks[6], (L, D_MODEL)),
        "w_in": nrm(ks[7], (L, D_MODEL, IN_DIM), D_MODEL),
        "mla_q_norm_w": gain(ks[8], (L, MLA_Q_LORA)),
        "mla_w_uq": nrm(ks[9], (L, MLA_Q_LORA, MLA_HEADS * MLA_QK), MLA_Q_LORA),
        "mla_kv_norm_w": gain(ks[10], (L, MLA_KV_LORA)),
        "mla_w_ukv": nrm(ks[11], (L, MLA_KV_LORA, MLA_HEADS * (MLA_NOPE + MLA_V)), MLA_KV_LORA),
        "mla_q_head_norm_w": gain(ks[12], (L, MLA_QK)),
        "mla_k_head_norm_w": gain(ks[13], (L, MLA_QK)),
        "mla_w_o": nrm(ks[14], (L, MLA_WIDTH, D_MODEL), MLA_WIDTH),
        "dn_conv_w": nrm(ks[15], (L, CONV_W, 2 * DN_WIDTH_K + DN_WIDTH), CONV_W),
        "dn_a_log": jnp.log(jax.random.uniform(ks[16], (L, N_DIR, DN_HEADS), f32, minval=1.0, maxval=16.0)),
        "dn_dt_bias": dt + jnp.log(-jnp.expm1(-dt)),
        "dn_out_norm_w": gain(ks[18], (L, DN_DV)),
        "dn_w_o": nrm(ks[19], (L, DN_WIDTH, D_MODEL), DN_WIDTH),
        "w_out": nrm(ks[20], (L, D_MODEL, D_MODEL), D_MODEL),
    }


def reference(x, c, ctx, c_ctx, w_mod, b_mod, norm_w, w_in, mla_q_norm_w, mla_w_uq, mla_kv_norm_w, mla_w_ukv,
              mla_q_head_norm_w, mla_k_head_norm_w, mla_w_o, dn_conv_w, dn_a_log, dn_dt_bias, dn_out_norm_w, dn_w_o,
              w_out):
    rope_cos, rope_sin = axial_rope_tables(x.shape[1], x.dtype)
    for layer in range(DEPTH):
        x, ctx = hybrid_layer(x, ctx, c, c_ctx, rope_cos, rope_sin, w_mod[layer], b_mod[layer], norm_w[layer],
                              w_in[layer], mla_q_norm_w[layer], mla_w_uq[layer], mla_kv_norm_w[layer],
                              mla_w_ukv[layer], mla_q_head_norm_w[layer], mla_k_head_norm_w[layer], mla_w_o[layer],
                              dn_conv_w[layer], dn_a_log[layer], dn_dt_bias[layer], dn_out_norm_w[layer],
                              dn_w_o[layer], w_out[layer], update_ctx=(layer < DEPTH - 1))
    return x
```

```python
import functools
import math

import numpy as np
import jax
import jax.numpy as jnp
from jax import lax
from jax.experimental import pallas as pl
from jax.experimental.pallas import tpu as pltpu

f32 = jnp.float32
bf16 = jnp.bfloat16

HEADS = 8
NOPE = 64
ROPE = 32
QK = NOPE + ROPE
VD = 64
DK = 64
N_DIR = 2
CONV_W = 5
CHUNK = 64
GRID_W = 64
ROPE_THETA = 10000.0
EPS = 1e-6

LANES = 128
VMEM_LIMIT = 56 * 1024 * 1024

_NT = (((1,), (1,)), ((), ()))
_TN = (((0,), (0,)), ((), ()))


def _cparams(sem):
    return pltpu.CompilerParams(dimension_semantics=sem, vmem_limit_bytes=VMEM_LIMIT)


def _dot(a, b):
    return jnp.dot(a.astype(bf16), b.astype(bf16), preferred_element_type=f32)


def _silu(x):
    return x * jax.nn.sigmoid(x)


def _mod_kernel(c_ref, w_ref, b_ref, o_ref):
    o_ref[...] = jnp.dot(_silu(c_ref[...]), w_ref[...], preferred_element_type=f32,
                         precision=lax.Precision.HIGHEST) + b_ref[...]


def _modulation(cc, w_mod, b_mod):
    R, D = cc.shape
    N = w_mod.shape[1]
    nb = N // D
    return pl.pallas_call(
        _mod_kernel,
        grid=(nb,),
        in_specs=[pl.BlockSpec((R, D), lambda j: (0, 0)),
                  pl.BlockSpec((D, D), lambda j: (0, j)),
                  pl.BlockSpec((1, D), lambda j: (0, j))],
        out_specs=pl.BlockSpec((R, D), lambda j: (0, j)),
        out_shape=jax.ShapeDtypeStruct((R, N), f32),
        compiler_params=_cparams(("arbitrary",)),
        name="modulation",
    )(cc, w_mod, b_mod.reshape(1, N))


def _inproj_kernel(x_ref, mod_ref, nw_ref, w_ref, *out_refs, segs):
    x = x_ref[0]
    y = x * lax.rsqrt(jnp.mean(x * x, axis=-1, keepdims=True) + EPS) * nw_ref[...]
    h = (y * (1.0 + mod_ref[0, 1:2, :]) + mod_ref[0, 0:1, :]).astype(bf16)
    for (a, b), o_ref in zip(segs, out_refs):
        o_ref[0] = jnp.dot(h, w_ref[:, a:b], preferred_element_type=f32).astype(o_ref.dtype)


def _inproj(x, mod, norm_w, w, widths, dtypes, tm):
    B, L, D = x.shape
    offs = np.concatenate([[0], np.cumsum(widths)])
    segs = tuple((int(offs[i]), int(offs[i + 1])) for i in range(len(widths)))
    return pl.pallas_call(
        functools.partial(_inproj_kernel, segs=segs),
        grid=(B, L // tm),
        in_specs=[pl.BlockSpec((1, tm, D), lambda b, t: (b, t, 0)),
                  pl.BlockSpec((1, 8, D), lambda b, t: (b, 0, 0)),
                  pl.BlockSpec((1, D), lambda b, t: (0, 0)),
                  pl.BlockSpec(w.shape, lambda b, t: (0, 0))],
        out_specs=[pl.BlockSpec((1, tm, n), lambda b, t: (b, t, 0)) for n in widths],
        out_shape=[jax.ShapeDtypeStruct((B, L, n), dt) for n, dt in zip(widths, dtypes)],
        compiler_params=_cparams(("arbitrary", "arbitrary")),
        name="inproj",
    )(x, mod, norm_w.reshape(1, D), w)


def _head_norm_rope(x, hw, tabs):
    ss = jnp.sum(x * x, axis=-1, keepdims=True) * (1.0 / QK)
    xn = x * lax.rsqrt(ss + EPS) * hw
    if tabs is None:
        return xn
    c, sp, sm = tabs
    return xn * c + pltpu.roll(xn, 8, 1) * sp + pltpu.roll(xn, LANES - 8, 1) * sm


def _qprep_kernel(cq_ref, nw_ref, w_ref, hw_ref, c_ref, sp_ref, sm_ref, o_ref):
    cq = cq_ref[0].astype(f32)
    cn = cq * lax.rsqrt(jnp.mean(cq * cq, axis=-1, keepdims=True) + EPS) * nw_ref[...]
    qa = _dot(cn, w_ref[...])
    tabs = (c_ref[...], sp_ref[...], sm_ref[...])
    for h in range(HEADS):
        sl = slice(h * LANES, (h + 1) * LANES)
        o_ref[0, :, sl] = _head_norm_rope(qa[:, sl], hw_ref[...], tabs).astype(o_ref.dtype)


def _kvprep_kernel(ckv_ref, kr_ref, nw_ref, wk_ref, wv_ref, hw_ref, *rest, rope):
    if rope:
        c_ref, sp_ref, sm_ref, k_ref, v_ref = rest
        tabs = (c_ref[...], sp_ref[...], sm_ref[...])
    else:
        k_ref, v_ref = rest
        tabs = None
    ckv = ckv_ref[0].astype(f32)
    cn = (ckv * lax.rsqrt(jnp.mean(ckv * ckv, axis=-1, keepdims=True) + EPS) * nw_ref[...]).astype(bf16)
    v_ref[0] = jnp.dot(cn, wv_ref[...], preferred_element_type=f32).astype(v_ref.dtype)
    ka = jnp.dot(cn, wk_ref[...], preferred_element_type=f32)
    kr = kr_ref[0]
    for h in range(HEADS):
        sl = slice(h * LANES, (h + 1) * LANES)
        k_ref[0, :, sl] = _head_norm_rope(ka[:, sl] + kr, hw_ref[...], tabs).astype(k_ref.dtype)


def _qprep(cq, nw, wq, hw, tabs, tm):
    B, L, C = cq.shape
    N = wq.shape[1]
    tab_spec = pl.BlockSpec((tm, LANES), lambda b, t: (t, 0))
    return pl.pallas_call(
        _qprep_kernel,
        grid=(B, L // tm),
        in_specs=[pl.BlockSpec((1, tm, C), lambda b, t: (b, t, 0)),
                  pl.BlockSpec((1, C), lambda b, t: (0, 0)),
                  pl.BlockSpec(wq.shape, lambda b, t: (0, 0)),
                  pl.BlockSpec((1, LANES), lambda b, t: (0, 0)),
                  tab_spec, tab_spec, tab_spec],
        out_specs=pl.BlockSpec((1, tm, N), lambda b, t: (b, t, 0)),
        out_shape=jax.ShapeDtypeStruct((B, L, N), bf16),
        compiler_params=_cparams(("arbitrary", "arbitrary")),
        name="mla_q_prep",
    )(cq, nw.reshape(1, C), wq, hw, *tabs)


def _kvprep(ckv, kr, nw, wk, wv, hw, tabs, tm):
    B, L, C = ckv.shape
    rope = tabs is not None
    tab_spec = pl.BlockSpec((tm, LANES), lambda b, t: (t, 0))
    in_specs = [pl.BlockSpec((1, tm, C), lambda b, t: (b, t, 0)),
                pl.BlockSpec((1, tm, LANES), lambda b, t: (b, t, 0)),
                pl.BlockSpec((1, C), lambda b, t: (0, 0)),
                pl.BlockSpec(wk.shape, lambda b, t: (0, 0)),
                pl.BlockSpec(wv.shape, lambda b, t: (0, 0)),
                pl.BlockSpec((1, LANES), lambda b, t: (0, 0))]
    args = [ckv, kr, nw.reshape(1, C), wk, wv, hw]
    if rope:
        in_specs += [tab_spec] * 3
        args += list(tabs)
    return pl.pallas_call(
        functools.partial(_kvprep_kernel, rope=rope),
        grid=(B, L // tm),
        in_specs=in_specs,
        out_specs=[pl.BlockSpec((1, tm, wk.shape[1]), lambda b, t: (b, t, 0)),
                   pl.BlockSpec((1, tm, wv.shape[1]), lambda b, t: (b, t, 0))],
        out_shape=[jax.ShapeDtypeStruct((B, L, wk.shape[1]), bf16),
                   jax.ShapeDtypeStruct((B, L, wv.shape[1]), bf16)],
        compiler_params=_cparams(("arbitrary", "arbitrary")),
        name="mla_kv_prep_rope" if rope else "mla_kv_prep",
    )(*args)


def _attn_kernel(q_ref, kc_ref, kl_ref, vc_ref, vl_ref, o_ref, s_ref, *, kv_chunk):
    lc = kc_ref.shape[1]
    ll = kl_ref.shape[1]
    chunks = [(kc_ref, vc_ref, 0, lc, 0)]
    for off in range(0, ll, kv_chunk):
        chunks.append((kl_ref, vl_ref, off, kv_chunk, lc + off))
    outs = []
    for a in range(2):
        sl = slice(a * LANES, (a + 1) * LANES)
        q = q_ref[0, :, sl]
        m = None
        for k_ref, _, off, n, col in chunks:
            s = lax.dot_general(q, k_ref[0, off:off + n, sl], _NT, preferred_element_type=f32)
            s_ref[:, col:col + n] = s
            cm = jnp.max(s, axis=-1, keepdims=True)
            m = cm if m is None else jnp.maximum(m, cm)
        l = None
        acc = None
        for _, v_ref, off, n, col in chunks:
            p = jnp.exp2(s_ref[:, col:col + n] - m)
            ps = jnp.sum(p, axis=-1, keepdims=True)
            pv = jnp.dot(p.astype(bf16), v_ref[0, off:off + n, :], preferred_element_type=f32)
            l = ps if l is None else l + ps
            acc = pv if acc is None else acc + pv
        outs.append(acc / l)
    lane = lax.broadcasted_iota(jnp.int32, outs[0].shape, 1)
    o_ref[0] = jnp.where(lane < VD, outs[0], outs[1]).astype(o_ref.dtype)


def _attention(q, k_ctx, k_lat, v_ctx, v_lat, tq, kv_chunk):
    B, T, _ = q.shape
    lc, ll = k_ctx.shape[1], k_lat.shape[1]
    pairs = HEADS // 2
    return pl.pallas_call(
        functools.partial(_attn_kernel, kv_chunk=kv_chunk),
        grid=(B, pairs, T // tq),
        in_specs=[pl.BlockSpec((1, tq, 2 * LANES), lambda b, p, t: (b, t, p)),
                  pl.BlockSpec((1, lc, 2 * LANES), lambda b, p, t: (b, 0, p)),
                  pl.BlockSpec((1, ll, 2 * LANES), lambda b, p, t: (b, 0, p)),
                  pl.BlockSpec((1, lc, LANES), lambda b, p, t: (b, 0, p)),
                  pl.BlockSpec((1, ll, LANES), lambda b, p, t: (b, 0, p))],
        out_specs=pl.BlockSpec((1, tq, LANES), lambda b, p, t: (b, t, p)),
        out_shape=jax.ShapeDtypeStruct((B, T, HEADS * VD), bf16),
        scratch_shapes=[pltpu.VMEM((tq, lc + ll), f32)],
        compiler_params=_cparams(("arbitrary", "arbitrary", "arbitrary")),
        name="mla_attention",
    )(q, k_ctx, k_lat, v_ctx, v_lat)


def _conv_kernel(xc_ref, xl_ref, w_ref, o_ref, pad_ref, *, tile):
    j = pl.program_id(1)
    nq = (HEADS * DK) // LANES
    lane = lax.broadcasted_iota(jnp.int32, (tile, LANES), 1)
    lo = lane < DK
    qscale = jnp.where(j < nq, DK ** -0.5, 1.0).astype(f32)
    is_qk = j < 2 * nq
    w = w_ref[...]
    half = CONV_W // 2
    row0 = 0
    for x_ref in (xc_ref, xl_ref):
        L = x_ref.shape[1]
        pad_ref[0:8, :] = jnp.zeros((8, LANES), f32)
        pad_ref[8:8 + L, :] = x_ref[0]
        pad_ref[8 + L:16 + L, :] = jnp.zeros((8, LANES), f32)
        for t0 in range(0, L, tile):
            y = None
            for tap in range(CONV_W):
                term = pad_ref[pl.ds(t0 + 8 - half + tap, tile), :] * w[tap:tap + 1, :]
                y = term if y is None else y + term
            y = _silu(y)
            y2 = y * y
            s_lo = jnp.sum(jnp.where(lo, y2, 0.0), axis=-1, keepdims=True)
            s_hi = jnp.sum(jnp.where(lo, 0.0, y2), axis=-1, keepdims=True)
            r = lax.rsqrt(jnp.where(lo, s_lo, s_hi) + EPS) * qscale
            o_ref[0, row0 + t0:row0 + t0 + tile, :] = y * jnp.where(is_qk, r, 1.0)
        row0 += L


def _dn_conv(x_ctx, x_lat, conv_w8, tile):
    B, lc, C = x_ctx.shape
    ll = x_lat.shape[1]
    return pl.pallas_call(
        functools.partial(_conv_kernel, tile=tile),
        grid=(B, C // LANES),
        in_specs=[pl.BlockSpec((1, lc, LANES), lambda b, j: (b, 0, j)),
                  pl.BlockSpec((1, ll, LANES), lambda b, j: (b, 0, j)),
                  pl.BlockSpec((8, LANES), lambda b, j: (0, j))],
        out_specs=pl.BlockSpec((1, lc + ll, LANES), lambda b, j: (b, 0, j)),
        out_shape=jax.ShapeDtypeStruct((B, lc + ll, C), f32),
        scratch_shapes=[pltpu.VMEM((max(lc, ll) + 16, LANES), f32)],
        compiler_params=_cparams(("arbitrary", "arbitrary")),
        name="dn_conv",
    )(x_ctx, x_lat, conv_w8)


def _gates_kernel(gc_ref, gl_ref, al_ref, dt_ref, beta_ref, gcum_ref, egc_ref, etg_ref):
    nh = N_DIR * HEADS
    row0 = 0
    for g_ref in (gc_ref, gl_ref):
        L = g_ref.shape[1]
        x = g_ref[0]
        rows = slice(row0, row0 + L)
        beta_ref[0, rows, :] = jax.nn.sigmoid(x)
        z = x + dt_ref[...]
        g = -jnp.exp(al_ref[...]) * (jnp.maximum(z, 0.0) + jnp.log1p(jnp.exp(-jnp.abs(z))))
        pos = lax.broadcasted_iota(jnp.int32, (L, LANES), 0) % CHUNK
        pre = g
        suf = g
        s = 1
        while s < CHUNK:
            pre = pre + jnp.where(pos >= s, pltpu.roll(pre, s, 0), 0.0)
            suf = suf + jnp.where(pos < CHUNK - s, pltpu.roll(suf, L - s, 0), 0.0)
            s *= 2
        lane = lax.broadcasted_iota(jnp.int32, (L, LANES), 1)
        fwd = lane < nh + HEADS
        gcum = jnp.where(fwd, pre, suf)
        tot = pre + suf - g
        gcum_ref[0, rows, :] = gcum
        egc_ref[0, rows, :] = jnp.exp(gcum)
        etg_ref[0, rows, :] = jnp.exp(tot - gcum)
        row0 += L


def _dn_gates(g_ctx, g_lat, a_lanes, dt_lanes):
    B, lc, _ = g_ctx.shape
    ll = g_lat.shape[1]
    L = lc + ll
    out = jax.ShapeDtypeStruct((B, L, LANES), f32)
    ospec = pl.BlockSpec((1, L, LANES), lambda b: (b, 0, 0))
    return pl.pallas_call(
        _gates_kernel,
        grid=(B,),
        in_specs=[pl.BlockSpec((1, lc, LANES), lambda b: (b, 0, 0)),
                  pl.BlockSpec((1, ll, LANES), lambda b: (b, 0, 0)),
                  pl.BlockSpec((1, LANES), lambda b: (0, 0)),
                  pl.BlockSpec((1, LANES), lambda b: (0, 0))],
        out_specs=[ospec] * 4,
        out_shape=[out] * 4,
        compiler_params=_cparams(("arbitrary",)),
        name="dn_gates",
    )(g_ctx, g_lat, a_lanes, dt_lanes)


def _dn_kernel(q_ref, k_ref, v_ref, cg_ref, gr_ref, o_ref, u_s, w_s, qd_s, kd_s, qk_s, et_s, oacc_s, *, n_ctx):
    n_chunks = q_ref.shape[1] // CHUNK
    C = CHUNK
    row = lax.broadcasted_iota(jnp.int32, (C, LANES), 0)
    lane = lax.broadcasted_iota(jnp.int32, (C, LANES), 1)
    col = lane % C
    la = lane < C
    same16 = (row // 16) == (col // 16)
    eye = jnp.where(row == col, 1.0, 0.0).astype(f32)
    r2 = lax.broadcasted_iota(jnp.int32, (2 * C, LANES), 0)
    l2 = lax.broadcasted_iota(jnp.int32, (2 * C, LANES), 1)
    bd_mask = (r2 < C) == (l2 < C)

    def bd(y):
        return jnp.where(bd_mask, jnp.concatenate([y, y], axis=0), 0.0).astype(bf16)

    def mm(x, y):
        return jnp.dot(x.astype(bf16), bd(y), preferred_element_type=f32)

    for d in range(N_DIR):
        incl = (row >= col) if d == 0 else (row <= col)
        strict = (row > col) if d == 0 else (row < col)
        last = C - 1 if d == 0 else 0

        def precompute(c, carry, d=d, incl=incl, strict=strict, last=last):
            r0 = pl.multiple_of(c * C, C)
            q = q_ref[0, pl.ds(r0, C), :]
            k = k_ref[0, pl.ds(r0, C), :]
            v = v_ref[0, pl.ds(r0, C), :]
            cg = cg_ref[0, 0, pl.ds(r0, C), :]

            def colb(i):
                i = d * 8 + i
                return jnp.where(la, cg[:, i:i + 1], cg[:, i + 1:i + 2])

            beta, gcol, egc, etg = colb(0), colb(2), colb(4), colb(6)
            grow = gr_ref[0, 0, d, pl.ds(c, 1), :]
            diff = gcol - grow
            decay = jnp.where(incl, jnp.exp(jnp.where(incl, diff, 0.0)), 0.0)
            kb = k * beta
            kkqk = lax.dot_general(jnp.concatenate([kb, q], axis=0).astype(bf16), bd(k), _NT,
                                   preferred_element_type=f32)
            lm = jnp.where(strict, kkqk[:C] * decay, 0.0)
            qkm = jnp.where(incl, kkqk[C:] * decay, 0.0)
            dg = jnp.where(same16, lm, 0.0)
            e = lm - dg
            d2 = mm(dg, dg)
            d4 = mm(d2, d2)
            d8 = mm(d4, d4)
            x1 = eye - dg + d2 - mm(dg, d2)
            x2 = eye + d4 + d8 + mm(d4, d8)
            t16 = mm(x1, x2)
            n1 = mm(t16, e)
            n2 = mm(n1, n1)
            tinv = mm(eye - n1 + n2 - mm(n1, n2), t16)
            vb = v * beta
            kbe = kb * egc
            rhs = jnp.concatenate([jnp.concatenate([vb, kbe], axis=1)] * 2, axis=0)
            rmask = jnp.concatenate([bd_mask, bd_mask], axis=1)
            uw = jnp.dot(tinv.astype(bf16), jnp.where(rmask, rhs, 0.0).astype(bf16), preferred_element_type=f32)
            u_s[pl.ds(r0, C), :] = uw[:, :LANES]
            w_s[pl.ds(r0, C), :] = uw[:, LANES:]
            qd_s[pl.ds(r0, C), :] = q * egc
            kd_s[pl.ds(r0, C), :] = k * etg
            qk_s[pl.ds(r0, C), :] = qkm
            et_s[pl.ds(c, 1), :] = egc[last:last + 1, :]
            return carry

        lax.fori_loop(0, n_chunks, precompute, 0)

        def scan(n, state, d=d):
            if d == 0:
                c = n
            else:
                c = jnp.where(n < n_ctx, n_ctx - 1 - n, n_chunks + n_ctx - 1 - n)
            r0 = pl.multiple_of(c * C, C)
            u = u_s[pl.ds(r0, C), :]
            lhs = jnp.concatenate([w_s[pl.ds(r0, C), :], qd_s[pl.ds(r0, C), :]], axis=0)
            ws = jnp.dot(lhs.astype(bf16), state.astype(bf16), preferred_element_type=f32)
            v_new = u - ws[:C]
            o = ws[C:] + mm(qk_s[pl.ds(r0, C), :], v_new)
            if d == 0:
                oacc_s[pl.ds(r0, C), :] = o
            else:
                oacc_s[pl.ds(r0, C), :] = oacc_s[pl.ds(r0, C), :] + o
            upd = lax.dot_general(kd_s[pl.ds(r0, C), :].astype(bf16), v_new.astype(bf16), _TN,
                                  preferred_element_type=f32)
            return state * et_s[pl.ds(c, 1), :] + jnp.where(bd_mask, upd, 0.0)

        lax.fori_loop(0, n_chunks, scan, jnp.zeros((2 * C, LANES), f32))

    o_ref[0] = oacc_s[n_ctx * C:, :]


def _dn_delta(qkv, cg, grow, n_ctx):
    B, L, _ = qkv.shape
    pairs = HEADS // 2
    ll = L - n_ctx * CHUNK
    nb = (HEADS * DK) // LANES
    big = pltpu.VMEM((L, LANES), f32)
    return pl.pallas_call(
        functools.partial(_dn_kernel, n_ctx=n_ctx),
        grid=(B, pairs),
        in_specs=[pl.BlockSpec((1, L, LANES), lambda b, p: (b, 0, p)),
                  pl.BlockSpec((1, L, LANES), lambda b, p: (b, 0, nb + p)),
                  pl.BlockSpec((1, L, LANES), lambda b, p: (b, 0, 2 * nb + p)),
                  pl.BlockSpec((1, 1, L, cg.shape[-1]), lambda b, p: (b, p, 0, 0)),
                  pl.BlockSpec((1, 1, N_DIR, L // CHUNK, LANES), lambda b, p: (b, p, 0, 0, 0))],
        out_specs=pl.BlockSpec((1, ll, LANES), lambda b, p: (b, 0, p)),
        out_shape=jax.ShapeDtypeStruct((B, ll, HEADS * DK), f32),
        scratch_shapes=[big, big, big, big, big, pltpu.VMEM((L // CHUNK, LANES), f32), big],
        compiler_params=_cparams(("arbitrary", "arbitrary")),
        name="dn_delta",
    )(qkv, qkv, qkv, cg, grow)


def _merge_kernel(x_ref, mod_ref, om_ref, zm_ref, od_ref, zd_ref, g_ref, wmo_ref, dnw_ref, wdo_ref, wout_ref, o_ref):
    ym = _dot(om_ref[0].astype(f32) * _silu(zm_ref[0].astype(f32)), wmo_ref[...])
    od = od_ref[0]
    lane = lax.broadcasted_iota(jnp.int32, (od.shape[0], LANES), 1)
    lo = lane < DK
    parts = []
    for j in range(od.shape[1] // LANES):
        y = od[:, j * LANES:(j + 1) * LANES]
        y2 = y * y
        s_lo = jnp.sum(jnp.where(lo, y2, 0.0), axis=-1, keepdims=True)
        s_hi = jnp.sum(jnp.where(lo, 0.0, y2), axis=-1, keepdims=True)
        parts.append(y * lax.rsqrt(jnp.where(lo, s_lo, s_hi) * (1.0 / DK) + EPS))
    odn = jnp.concatenate(parts, axis=1) * dnw_ref[...]
    yd = _dot(odn * _silu(zd_ref[0].astype(f32)), wdo_ref[...])
    D = ym.shape[1]
    g = g_ref[0].astype(f32)
    y = _dot(jax.nn.sigmoid(g[:, :D]) * ym + jax.nn.sigmoid(g[:, D:]) * yd, wout_ref[...])
    o_ref[0] = x_ref[0] + mod_ref[0, 2:3, :] * y


def _merge(x, mod, o_mla, zm, o_dn, zd, gates, w_mo, dn_nw, w_do, w_out, tm):
    B, T, D = x.shape
    W = o_mla.shape[-1]
    tok = lambda n: pl.BlockSpec((1, tm, n), lambda b, t: (b, t, 0))
    full = lambda a: pl.BlockSpec(a.shape, lambda b, t: (0, 0))
    return pl.pallas_call(
        _merge_kernel,
        grid=(B, T // tm),
        in_specs=[tok(D), pl.BlockSpec((1, 8, D), lambda b, t: (b, 0, 0)),
                  tok(W), tok(W), tok(W), tok(W), tok(2 * D),
                  full(w_mo), full(dn_nw), full(w_do), full(w_out)],
        out_specs=tok(D),
        out_shape=jax.ShapeDtypeStruct((B, T, D), f32),
        compiler_params=_cparams(("arbitrary", "arbitrary")),
        name="merge",
    )(x, mod, o_mla, zm, o_dn, zd, gates, w_mo, dn_nw, w_do, w_out)


def _rope_tables(T, q_scale):
    half = ROPE // 4
    axis_dims = ROPE // 2
    rows = T // GRID_W
    inv_freq = ROPE_THETA ** (-jnp.arange(0, axis_dims, 2, dtype=f32) / axis_dims)
    rowp = jnp.repeat(jnp.arange(rows, dtype=f32), GRID_W)[:, None] * inv_freq
    colp = jnp.tile(jnp.arange(GRID_W, dtype=f32), rows)[:, None] * inv_freq
    cos = jnp.concatenate([jnp.cos(rowp)] * 2 + [jnp.cos(colp)] * 2, axis=1)
    sin = jnp.concatenate([jnp.sin(rowp)] * 2 + [jnp.sin(colp)] * 2, axis=1)
    first = np.tile(np.concatenate([np.ones(half), np.zeros(half)]), 2).astype(np.float32)
    zl = jnp.zeros((T, LANES - QK), f32)
    c = jnp.concatenate([jnp.ones((T, NOPE), f32), cos, zl], axis=1)
    sp = jnp.concatenate([jnp.zeros((T, NOPE), f32), sin * (1.0 - first), zl], axis=1)
    sm = jnp.concatenate([jnp.zeros((T, NOPE), f32), -sin * first, zl], axis=1)
    return c, sp, sm, (c * q_scale, sp * q_scale, sm * q_scale)


def _pad_heads(w, per_head, keep):
    K = w.shape[0]
    wh = w.reshape(K, HEADS, per_head)[:, :, :keep]
    return jnp.pad(wh, ((0, 0), (0, 0), (0, LANES - keep))).reshape(K, HEADS * LANES)


def kernel(x, c, ctx, c_ctx, w_mod, b_mod, norm_w, w_in, mla_q_norm_w, mla_w_uq, mla_kv_norm_w, mla_w_ukv,
           mla_q_head_norm_w, mla_k_head_norm_w, mla_w_o, dn_conv_w, dn_a_log, dn_dt_bias, dn_out_norm_w, dn_w_o,
           w_out):
    B, T, D = x.shape
    LC = ctx.shape[1]
    assert w_mod.shape[0] == 1, "one layer"
    (w_mod, b_mod, norm_w, w_in, mla_q_norm_w, mla_w_uq, mla_kv_norm_w, mla_w_ukv, mla_q_head_norm_w,
     mla_k_head_norm_w, mla_w_o, dn_conv_w, dn_a_log, dn_dt_bias, dn_out_norm_w, dn_w_o, w_out) = (
        a[0] for a in (w_mod, b_mod, norm_w, w_in, mla_q_norm_w, mla_w_uq, mla_kv_norm_w, mla_w_ukv,
                       mla_q_head_norm_w, mla_k_head_norm_w, mla_w_o, dn_conv_w, dn_a_log, dn_dt_bias,
                       dn_out_norm_w, dn_w_o, w_out))
    QL, KVL = mla_w_uq.shape[0], mla_w_ukv.shape[0]
    WM, WK = HEADS * VD, HEADS * DK
    nh = N_DIR * HEADS

    R = -(-(B + 1) // 8) * 8
    cc = jnp.concatenate([c, c_ctx[None], jnp.zeros((R - B - 1, D), f32)], axis=0)
    mod = _modulation(cc, w_mod, b_mod).reshape(R, 3, D)
    mod8 = jnp.pad(mod, ((0, 0), (0, 5), (0, 0)))
    mod_lat = mod8[:B]
    mod_ctx = jnp.broadcast_to(mod8[B:B + 1], (B, 8, D))

    o = np.cumsum([0, QL, KVL, ROPE, WM, WK, WK, WK, WK, nh, nh, 2 * D])
    wcol = lambda i: w_in[:, o[i]:o[i + 1]]
    zeros = lambda n: jnp.zeros((D, n), f32)
    w_kr = jnp.concatenate([zeros(NOPE), wcol(2), zeros(LANES - QK)], axis=1)
    w_g = jnp.concatenate([wcol(8), wcol(9), zeros(LANES - 2 * nh)], axis=1)
    w_qkv = jnp.concatenate([wcol(4), wcol(5), wcol(6)], axis=1)
    w_lat = jnp.concatenate([wcol(0), wcol(1), w_kr, wcol(3), w_qkv, wcol(7), w_g, wcol(10)], axis=1).astype(bf16)
    w_ctx = jnp.concatenate([wcol(1), w_kr, w_qkv, w_g], axis=1).astype(bf16)
    cq, ckv_l, kr_l, zm, qkv_l, zd, g_l, gates = _inproj(
        x, mod_lat, norm_w, w_lat, [QL, KVL, LANES, WM, 3 * WK, WK, LANES, 2 * D],
        [bf16, bf16, f32, bf16, f32, bf16, f32, bf16], 512)
    ckv_c, kr_c, qkv_c, g_c = _inproj(ctx, mod_ctx, norm_w, w_ctx, [KVL, LANES, 3 * WK, LANES],
                                      [bf16, f32, f32, f32], LC)

    q_scale = (QK ** -0.5) * math.log2(math.e)
    c_t, sp_t, sm_t, q_tabs = _rope_tables(T, q_scale)
    wq = _pad_heads(mla_w_uq, QK, QK).astype(bf16)
    wk = _pad_heads(mla_w_ukv, NOPE + VD, NOPE).astype(bf16)
    wv = mla_w_ukv.reshape(KVL, HEADS, NOPE + VD)[:, :, NOPE:].reshape(KVL, WM).astype(bf16)
    hw_q = jnp.pad(mla_q_head_norm_w, (0, LANES - QK)).reshape(1, LANES)
    hw_k = jnp.pad(mla_k_head_norm_w, (0, LANES - QK)).reshape(1, LANES)
    q = _qprep(cq, mla_q_norm_w, wq, hw_q, q_tabs, 512)
    k_lat, v_lat = _kvprep(ckv_l, kr_l, mla_kv_norm_w, wk, wv, hw_k, (c_t, sp_t, sm_t), 512)
    k_ctx, v_ctx = _kvprep(ckv_c, kr_c, mla_kv_norm_w, wk, wv, hw_k, None, LC)
    o_mla = _attention(q, k_ctx, k_lat, v_ctx, v_lat, 512, 512)

    conv_w8 = jnp.pad(dn_conv_w, ((0, 8 - CONV_W), (0, 0)))
    qkv = _dn_conv(qkv_c, qkv_l, conv_w8, 256)
    a_lanes = jnp.pad(dn_a_log.reshape(1, nh), ((0, 0), (nh, LANES - 2 * nh)))
    dt_lanes = jnp.pad(dn_dt_bias.reshape(1, nh), ((0, 0), (nh, LANES - 2 * nh)))
    beta, gcum, egc, etg = _dn_gates(g_c, g_l, a_lanes, dt_lanes)
    L = LC + T
    pairs = HEADS // 2
    quant = jnp.stack([beta[..., :nh], gcum[..., nh:2 * nh], egc[..., nh:2 * nh], etg[..., nh:2 * nh]], axis=2)
    cg = quant.reshape(B, L, 4, N_DIR, pairs, 2).transpose(0, 4, 1, 3, 2, 5).reshape(B, pairs, L, N_DIR * 8)
    grow = gcum[..., nh:2 * nh].reshape(B, L // CHUNK, CHUNK, N_DIR, pairs, 2)
    grow = grow.transpose(0, 4, 3, 1, 5, 2).reshape(B, pairs, N_DIR, L // CHUNK, 2 * CHUNK)
    o_dn = _dn_delta(qkv, cg, grow, LC // CHUNK)

    dn_nw = jnp.tile(dn_out_norm_w, HEADS).reshape(1, WK)
    return _merge(x, mod_lat, o_mla, zm, o_dn, zd, gates, mla_w_o.astype(bf16), dn_nw, dn_w_o.astype(bf16),
                  w_out.astype(bf16), 512)
```

```python
import functools
import math

import numpy as np
import jax
import jax.numpy as jnp
from jax import lax
from jax.experimental import pallas as pl
from jax.experimental.pallas import tpu as pltpu

f32 = jnp.float32
bf16 = jnp.bfloat16

HEADS = 8
NOPE = 64
ROPE = 32
QK = NOPE + ROPE
VD = 64
DK = 64
N_DIR = 2
CONV_W = 5
CHUNK = 64
GRID_W = 64
ROPE_THETA = 10000.0
EPS = 1e-6

LANES = 128
VMEM_LIMIT = 56 * 1024 * 1024

_NT = (((1,), (1,)), ((), ()))
_TN = (((0,), (0,)), ((), ()))


def _cparams(sem):
    return pltpu.CompilerParams(dimension_semantics=sem, vmem_limit_bytes=VMEM_LIMIT)


def _dot(a, b):
    return jnp.dot(a.astype(bf16), b.astype(bf16), preferred_element_type=f32)


def _silu(x):
    return x * jax.nn.sigmoid(x)


def _mod_kernel(c_ref, w_ref, b_ref, o_ref):
    o_ref[...] = jnp.dot(_silu(c_ref[...]), w_ref[...], preferred_element_type=f32,
                         precision=lax.Precision.HIGHEST) + b_ref[...]


def _modulation(cc, w_mod, b_mod):
    R, D = cc.shape
    N = w_mod.shape[1]
    nb = N // D
    return pl.pallas_call(
        _mod_kernel,
        grid=(nb,),
        in_specs=[pl.BlockSpec((R, D), lambda j: (0, 0)),
                  pl.BlockSpec((D, D), lambda j: (0, j)),
                  pl.BlockSpec((1, D), lambda j: (0, j))],
        out_specs=pl.BlockSpec((R, D), lambda j: (0, j)),
        out_shape=jax.ShapeDtypeStruct((R, N), f32),
        compiler_params=_cparams(("arbitrary",)),
        name="modulation",
    )(cc, w_mod, b_mod.reshape(1, N))


def _inproj_kernel(x_ref, mod_ref, nw_ref, w_ref, *out_refs, segs):
    x = x_ref[0]
    y = x * lax.rsqrt(jnp.mean(x * x, axis=-1, keepdims=True) + EPS) * nw_ref[...]
    h = (y * (1.0 + mod_ref[0, 1:2, :]) + mod_ref[0, 0:1, :]).astype(bf16)
    for (a, b), o_ref in zip(segs, out_refs):
        o_ref[0] = jnp.dot(h, w_ref[:, a:b], preferred_element_type=f32).astype(o_ref.dtype)


def _inproj(x, mod, norm_w, w, widths, dtypes, tm):
    B, L, D = x.shape
    offs = np.concatenate([[0], np.cumsum(widths)])
    segs = tuple((int(offs[i]), int(offs[i + 1])) for i in range(len(widths)))
    return pl.pallas_call(
        functools.partial(_inproj_kernel, segs=segs),
        grid=(B, L // tm),
        in_specs=[pl.BlockSpec((1, tm, D), lambda b, t: (b, t, 0)),
                  pl.BlockSpec((1, 8, D), lambda b, t: (b, 0, 0)),
                  pl.BlockSpec((1, D), lambda b, t: (0, 0)),
                  pl.BlockSpec(w.shape, lambda b, t: (0, 0))],
        out_specs=[pl.BlockSpec((1, tm, n), lambda b, t: (b, t, 0)) for n in widths],
        out_shape=[jax.ShapeDtypeStruct((B, L, n), dt) for n, dt in zip(widths, dtypes)],
        compiler_params=_cparams(("arbitrary", "arbitrary")),
        name="inproj",
    )(x, mod, norm_w.reshape(1, D), w)


def _head_norm_rope(x, hw, tabs):
    ss = jnp.sum(x * x, axis=-1, keepdims=True) * (1.0 / QK)
    xn = x * lax.rsqrt(ss + EPS) * hw
    if tabs is None:
        return xn
    c, sp, sm = tabs
    return xn * c + pltpu.roll(xn, 8, 1) * sp + pltpu.roll(xn, LANES - 8, 1) * sm


def _qprep_kernel(cq_ref, nw_ref, w_ref, hw_ref, c_ref, sp_ref, sm_ref, o_ref):
    cq = cq_ref[0].astype(f32)
    cn = cq * lax.rsqrt(jnp.mean(cq * cq, axis=-1, keepdims=True) + EPS) * nw_ref[...]
    qa = _dot(cn, w_ref[...])
    tabs = (c_ref[...], sp_ref[...], sm_ref[...])
    for h in range(HEADS):
        sl = slice(h * LANES, (h + 1) * LANES)
        o_ref[0, :, sl] = _head_norm_rope(qa[:, sl], hw_ref[...], tabs).astype(o_ref.dtype)


def _kvprep_kernel(ckv_ref, kr_ref, nw_ref, wk_ref, wv_ref, hw_ref, *rest, rope):
    if rope:
        c_ref, sp_ref, sm_ref, k_ref, v_ref = rest
        tabs = (c_ref[...], sp_ref[...], sm_ref[...])
    else:
        k_ref, v_ref = rest
        tabs = None
    ckv = ckv_ref[0].astype(f32)
    cn = (ckv * lax.rsqrt(jnp.mean(ckv * ckv, axis=-1, keepdims=True) + EPS) * nw_ref[...]).astype(bf16)
    v_ref[0] = jnp.dot(cn, wv_ref[...], preferred_element_type=f32).astype(v_ref.dtype)
    ka = jnp.dot(cn, wk_ref[...], preferred_element_type=f32)
    kr = kr_ref[0]
    for h in range(HEADS):
        sl = slice(h * LANES, (h + 1) * LANES)
        k_ref[0, :, sl] = _head_norm_rope(ka[:, sl] + kr, hw_ref[...], tabs).astype(k_ref.dtype)


def _qprep(cq, nw, wq, hw, tabs, tm):
    B, L, C = cq.shape
    N = wq.shape[1]
    tab_spec = pl.BlockSpec((tm, LANES), lambda b, t: (t, 0))
    return pl.pallas_call(
        _qprep_kernel,
        grid=(B, L // tm),
        in_specs=[pl.BlockSpec((1, tm, C), lambda b, t: (b, t, 0)),
                  pl.BlockSpec((1, C), lambda b, t: (0, 0)),
                  pl.BlockSpec(wq.shape, lambda b, t: (0, 0)),
                  pl.BlockSpec((1, LANES), lambda b, t: (0, 0)),
                  tab_spec, tab_spec, tab_spec],
        out_specs=pl.BlockSpec((1, tm, N), lambda b, t: (b, t, 0)),
        out_shape=jax.ShapeDtypeStruct((B, L, N), bf16),
        compiler_params=_cparams(("arbitrary", "arbitrary")),
        name="mla_q_prep",
    )(cq, nw.reshape(1, C), wq, hw, *tabs)


def _kvprep(ckv, kr, nw, wk, wv, hw, tabs, tm):
    B, L, C = ckv.shape
    rope = tabs is not None
    tab_spec = pl.BlockSpec((tm, LANES), lambda b, t: (t, 0))
    in_specs = [pl.BlockSpec((1, tm, C), lambda b, t: (b, t, 0)),
                pl.BlockSpec((1, tm, LANES), lambda b, t: (b, t, 0)),
                pl.BlockSpec((1, C), lambda b, t: (0, 0)),
                pl.BlockSpec(wk.shape, lambda b, t: (0, 0)),
                pl.BlockSpec(wv.shape, lambda b, t: (0, 0)),
                pl.BlockSpec((1, LANES), lambda b, t: (0, 0))]
    args = [ckv, kr, nw.reshape(1, C), wk, wv, hw]
    if rope:
        in_specs += [tab_spec] * 3
        args += list(tabs)
    return pl.pallas_call(
        functools.partial(_kvprep_kernel, rope=rope),
        grid=(B, L // tm),
        in_specs=in_specs,
        out_specs=[pl.BlockSpec((1, tm, wk.shape[1]), lambda b, t: (b, t, 0)),
                   pl.BlockSpec((1, tm, wv.shape[1]), lambda b, t: (b, t, 0))],
        out_shape=[jax.ShapeDtypeStruct((B, L, wk.shape[1]), bf16),
                   jax.ShapeDtypeStruct((B, L, wv.shape[1]), bf16)],
        compiler_params=_cparams(("arbitrary", "arbitrary")),
        name="mla_kv_prep_rope" if rope else "mla_kv_prep",
    )(*args)


def _attn_kernel(q_ref, kc_ref, kl_ref, vc_ref, vl_ref, o_ref, s_ref, *, kv_chunk):
    lc = kc_ref.shape[1]
    ll = kl_ref.shape[1]
    chunks = [(kc_ref, vc_ref, 0, lc, 0)]
    for off in range(0, ll, kv_chunk):
        chunks.append((kl_ref, vl_ref, off, kv_chunk, lc + off))
    outs = []
    for a in range(2):
        sl = slice(a * LANES, (a + 1) * LANES)
        q = q_ref[0, :, sl]
        m = None
        for k_ref, _, off, n, col in chunks:
            s = lax.dot_general(q, k_ref[0, off:off + n, sl], _NT, preferred_element_type=f32)
            s_ref[:, col:col + n] = s
            cm = jnp.max(s, axis=-1, keepdims=True)
            m = cm if m is None else jnp.maximum(m, cm)
        l = None
        acc = None
        for _, v_ref, off, n, col in chunks:
            p = jnp.exp2(s_ref[:, col:col + n] - m)
            ps = jnp.sum(p, axis=-1, keepdims=True)
            pv = jnp.dot(p.astype(bf16), v_ref[0, off:off + n, :], preferred_element_type=f32)
            l = ps if l is None else l + ps
            acc = pv if acc is None else acc + pv
        outs.append(acc / l)
    lane = lax.broadcasted_iota(jnp.int32, outs[0].shape, 1)
    o_ref[0] = jnp.where(lane < VD, outs[0], outs[1]).astype(o_ref.dtype)


def _attention(q, k_ctx, k_lat, v_ctx, v_lat, tq, kv_chunk):
    B, T, _ = q.shape
    lc, ll = k_ctx.shape[1], k_lat.shape[1]
    pairs = HEADS // 2
    return pl.pallas_call(
        functools.partial(_attn_kernel, kv_chunk=kv_chunk),
        grid=(B, pairs, T // tq),
        in_specs=[pl.BlockSpec((1, tq, 2 * LANES), lambda b, p, t: (b, t, p)),
                  pl.BlockSpec((1, lc, 2 * LANES), lambda b, p, t: (b, 0, p)),
                  pl.BlockSpec((1, ll, 2 * LANES), lambda b, p, t: (b, 0, p)),
                  pl.BlockSpec((1, lc, LANES), lambda b, p, t: (b, 0, p)),
                  pl.BlockSpec((1, ll, LANES), lambda b, p, t: (b, 0, p))],
        out_specs=pl.BlockSpec((1, tq, LANES), lambda b, p, t: (b, t, p)),
        out_shape=jax.ShapeDtypeStruct((B, T, HEADS * VD), bf16),
        scratch_shapes=[pltpu.VMEM((tq, lc + ll), f32)],
        compiler_params=_cparams(("arbitrary", "arbitrary", "arbitrary")),
        name="mla_attention",
    )(q, k_ctx, k_lat, v_ctx, v_lat)


def _conv_kernel(xc_ref, xl_ref, w_ref, o_ref, pad_ref, *, tile):
    j = pl.program_id(1)
    nq = (HEADS * DK) // LANES
    lane = lax.broadcasted_iota(jnp.int32, (tile, LANES), 1)
    lo = lane < DK
    qscale = jnp.where(j < nq, DK ** -0.5, 1.0).astype(f32)
    is_qk = j < 2 * nq
    w = w_ref[...]
    half = CONV_W // 2
    row0 = 0
    for x_ref in (xc_ref, xl_ref):
        L = x_ref.shape[1]
        pad_ref[0:8, :] = jnp.zeros((8, LANES), f32)
        pad_ref[8:8 + L, :] = x_ref[0]
        pad_ref[8 + L:16 + L, :] = jnp.zeros((8, LANES), f32)
        for t0 in range(0, L, tile):
            y = None
            for tap in range(CONV_W):
                term = pad_ref[pl.ds(t0 + 8 - half + tap, tile), :] * w[tap:tap + 1, :]
                y = term if y is None else y + term
            y = _silu(y)
            y2 = y * y
            s_lo = jnp.sum(jnp.where(lo, y2, 0.0), axis=-1, keepdims=True)
            s_hi = jnp.sum(jnp.where(lo, 0.0, y2), axis=-1, keepdims=True)
            r = lax.rsqrt(jnp.where(lo, s_lo, s_hi) + EPS) * qscale
            o_ref[0, row0 + t0:row0 + t0 + tile, :] = y * jnp.where(is_qk, r, 1.0)
        row0 += L


def _dn_conv(x_ctx, x_lat, conv_w8, tile):
    B, lc, C = x_ctx.shape
    ll = x_lat.shape[1]
    return pl.pallas_call(
        functools.partial(_conv_kernel, tile=tile),
        grid=(B, C // LANES),
        in_specs=[pl.BlockSpec((1, lc, LANES), lambda b, j: (b, 0, j)),
                  pl.BlockSpec((1, ll, LANES), lambda b, j: (b, 0, j)),
                  pl.BlockSpec((8, LANES), lambda b, j: (0, j))],
        out_specs=pl.BlockSpec((1, lc + ll, LANES), lambda b, j: (b, 0, j)),
        out_shape=jax.ShapeDtypeStruct((B, lc + ll, C), f32),
        scratch_shapes=[pltpu.VMEM((max(lc, ll) + 16, LANES), f32)],
        compiler_params=_cparams(("arbitrary", "arbitrary")),
        name="dn_conv",
    )(x_ctx, x_lat, conv_w8)


def _gates_kernel(gc_ref, gl_ref, al_ref, dt_ref, beta_ref, gcum_ref, egc_ref, etg_ref):
    nh = N_DIR * HEADS
    row0 = 0
    for g_ref in (gc_ref, gl_ref):
        L = g_ref.shape[1]
        x = g_ref[0]
        rows = slice(row0, row0 + L)
        beta_ref[0, rows, :] = jax.nn.sigmoid(x)
        z = x + dt_ref[...]
        g = -jnp.exp(al_ref[...]) * (jnp.maximum(z, 0.0) + jnp.log1p(jnp.exp(-jnp.abs(z))))
        pos = lax.broadcasted_iota(jnp.int32, (L, LANES), 0) % CHUNK
        pre = g
        suf = g
        s = 1
        while s < CHUNK:
            pre = pre + jnp.where(pos >= s, pltpu.roll(pre, s, 0), 0.0)
            suf = suf + jnp.where(pos < CHUNK - s, pltpu.roll(suf, L - s, 0), 0.0)
            s *= 2
        lane = lax.broadcasted_iota(jnp.int32, (L, LANES), 1)
        fwd = lane < nh + HEADS
        gcum = jnp.where(fwd, pre, suf)
        tot = pre + suf - g
        gcum_ref[0, rows, :] = gcum
        egc_ref[0, rows, :] = jnp.exp(gcum)
        etg_ref[0, rows, :] = jnp.exp(tot - gcum)
        row0 += L


def _dn_gates(g_ctx, g_lat, a_lanes, dt_lanes):
    B, lc, _ = g_ctx.shape
    ll = g_lat.shape[1]
    L = lc + ll
    out = jax.ShapeDtypeStruct((B, L, LANES), f32)
    ospec = pl.BlockSpec((1, L, LANES), lambda b: (b, 0, 0))
    return pl.pallas_call(
        _gates_kernel,
        grid=(B,),
        in_specs=[pl.BlockSpec((1, lc, LANES), lambda b: (b, 0, 0)),
                  pl.BlockSpec((1, ll, LANES), lambda b: (b, 0, 0)),
                  pl.BlockSpec((1, LANES), lambda b: (0, 0)),
                  pl.BlockSpec((1, LANES), lambda b: (0, 0))],
        out_specs=[ospec] * 4,
        out_shape=[out] * 4,
        compiler_params=_cparams(("arbitrary",)),
        name="dn_gates",
    )(g_ctx, g_lat, a_lanes, dt_lanes)


def _dn_kernel(q_ref, k_ref, v_ref, cg_ref, gr_ref, o_ref, u_s, w_s, qd_s, kd_s, qk_s, et_s, o_s, *, n_ctx, unroll):
    n_chunks = q_ref.shape[1] // CHUNK
    C = CHUNK
    row = lax.broadcasted_iota(jnp.int32, (C, LANES), 0)
    lane = lax.broadcasted_iota(jnp.int32, (C, LANES), 1)
    col = lane % C
    la = lane < C
    same16 = (row // 16) == (col // 16)
    eye = jnp.where(row == col, 1.0, 0.0).astype(f32)
    r2 = lax.broadcasted_iota(jnp.int32, (2 * C, LANES), 0)
    l2 = lax.broadcasted_iota(jnp.int32, (2 * C, LANES), 1)
    bd_mask = (r2 < C) == (l2 < C)
    rmask = jnp.concatenate([bd_mask, bd_mask], axis=1)

    def bd(y):
        return jnp.where(bd_mask, jnp.concatenate([y, y], axis=0), 0.0).astype(bf16)

    def mm(x, y):
        return jnp.dot(x.astype(bf16), bd(y), preferred_element_type=f32)

    def precompute(g, carry):
        chains = []
        for i in range(unroll):
            c = g * unroll + i
            r0 = pl.multiple_of(c * C, C)
            q = q_ref[0, pl.ds(r0, C), :]
            k = k_ref[0, pl.ds(r0, C), :]
            v = v_ref[0, pl.ds(r0, C), :]
            cg = cg_ref[0, 0, pl.ds(r0, C), :]
            kbd = bd(k)
            for d in range(N_DIR):
                colb = lambda i, d=d, cg=cg: jnp.where(la, cg[:, d * 8 + i:d * 8 + i + 1],
                                                       cg[:, d * 8 + i + 1:d * 8 + i + 2])
                chains.append(dict(c=c, r0=r0, d=d, q=q, k=k, v=v, kbd=kbd, beta=colb(0), gcol=colb(2),
                                   egc=colb(4), etg=colb(6), grow=gr_ref[0, 0, d, pl.ds(c, 1), :]))
        for s in chains:
            incl = (row >= col) if s["d"] == 0 else (row <= col)
            s["decay"] = jnp.where(incl, jnp.exp(jnp.where(incl, s["gcol"] - s["grow"], 0.0)), 0.0)
            s["kb"] = s["k"] * s["beta"]
        for s in chains:
            s["kkqk"] = lax.dot_general(jnp.concatenate([s["kb"], s["q"]], axis=0).astype(bf16), s["kbd"], _NT,
                                        preferred_element_type=f32)
        for s in chains:
            incl = (row >= col) if s["d"] == 0 else (row <= col)
            strict = (row > col) if s["d"] == 0 else (row < col)
            lm = jnp.where(strict, s["kkqk"][:C] * s["decay"], 0.0)
            s["qkm"] = jnp.where(incl, s["kkqk"][C:] * s["decay"], 0.0)
            s["dg"] = jnp.where(same16, lm, 0.0)
            s["e"] = lm - s["dg"]
        for s in chains:
            s["d2"] = mm(s["dg"], s["dg"])
        for s in chains:
            s["d4"] = mm(s["d2"], s["d2"])
            s["dd2"] = mm(s["dg"], s["d2"])
        for s in chains:
            s["d8"] = mm(s["d4"], s["d4"])
        for s in chains:
            s["d48"] = mm(s["d4"], s["d8"])
        for s in chains:
            x1 = eye - s["dg"] + s["d2"] - s["dd2"]
            x2 = eye + s["d4"] + s["d8"] + s["d48"]
            s["t16"] = mm(x1, x2)
        for s in chains:
            s["n1"] = mm(s["t16"], s["e"])
        for s in chains:
            s["n2"] = mm(s["n1"], s["n1"])
        for s in chains:
            s["n3"] = mm(s["n1"], s["n2"])
        for s in chains:
            s["tinv"] = mm(eye - s["n1"] + s["n2"] - s["n3"], s["t16"])
        for s in chains:
            kbe = s["kb"] * s["egc"]
            rhs = jnp.concatenate([jnp.concatenate([s["v"] * s["beta"], kbe], axis=1)] * 2, axis=0)
            s["uw"] = jnp.dot(s["tinv"].astype(bf16), jnp.where(rmask, rhs, 0.0).astype(bf16),
                              preferred_element_type=f32)
        for s in chains:
            d, r0 = s["d"], s["r0"]
            last = C - 1 if d == 0 else 0
            u_s[d, pl.ds(r0, C), :] = s["uw"][:, :LANES]
            w_s[d, pl.ds(r0, C), :] = s["uw"][:, LANES:]
            qd_s[d, pl.ds(r0, C), :] = s["q"] * s["egc"]
            kd_s[d, pl.ds(r0, C), :] = s["k"] * s["etg"]
            qk_s[d, pl.ds(r0, C), :] = s["qkm"]
            et_s[d, pl.ds(s["c"], 1), :] = s["egc"][last:last + 1, :]
        return carry

    lax.fori_loop(0, n_chunks // unroll, precompute, 0)

    def scan(n, states):
        cs = [n, jnp.where(n < n_ctx, n_ctx - 1 - n, n_chunks + n_ctx - 1 - n)]
        r0s = [pl.multiple_of(c * C, C) for c in cs]
        ws = [jnp.dot(jnp.concatenate([w_s[d, pl.ds(r0s[d], C), :], qd_s[d, pl.ds(r0s[d], C), :]],
                                      axis=0).astype(bf16), states[d].astype(bf16), preferred_element_type=f32)
              for d in range(N_DIR)]
        v_new = [u_s[d, pl.ds(r0s[d], C), :] - ws[d][:C] for d in range(N_DIR)]
        upd = [lax.dot_general(kd_s[d, pl.ds(r0s[d], C), :].astype(bf16), v_new[d].astype(bf16), _TN,
                               preferred_element_type=f32) for d in range(N_DIR)]
        for d in range(N_DIR):
            o_s[d, pl.ds(r0s[d], C), :] = ws[d][C:] + mm(qk_s[d, pl.ds(r0s[d], C), :], v_new[d])
        return tuple(states[d] * et_s[d, pl.ds(cs[d], 1), :] + jnp.where(bd_mask, upd[d], 0.0)
                     for d in range(N_DIR))

    zero = jnp.zeros((2 * C, LANES), f32)
    lax.fori_loop(0, n_chunks, scan, (zero, zero))
    o_ref[0] = o_s[0, n_ctx * C:, :] + o_s[1, n_ctx * C:, :]


def _dn_delta(qkv, cg, grow, n_ctx, unroll):
    B, L, _ = qkv.shape
    pairs = HEADS // 2
    ll = L - n_ctx * CHUNK
    nb = (HEADS * DK) // LANES
    big = pltpu.VMEM((N_DIR, L, LANES), f32)
    return pl.pallas_call(
        functools.partial(_dn_kernel, n_ctx=n_ctx, unroll=unroll),
        grid=(B, pairs),
        in_specs=[pl.BlockSpec((1, L, LANES), lambda b, p: (b, 0, p)),
                  pl.BlockSpec((1, L, LANES), lambda b, p: (b, 0, nb + p)),
                  pl.BlockSpec((1, L, LANES), lambda b, p: (b, 0, 2 * nb + p)),
                  pl.BlockSpec((1, 1, L, cg.shape[-1]), lambda b, p: (b, p, 0, 0)),
                  pl.BlockSpec((1, 1, N_DIR, L // CHUNK, LANES), lambda b, p: (b, p, 0, 0, 0))],
        out_specs=pl.BlockSpec((1, ll, LANES), lambda b, p: (b, 0, p)),
        out_shape=jax.ShapeDtypeStruct((B, ll, HEADS * DK), f32),
        scratch_shapes=[big, big, big, big, big, pltpu.VMEM((N_DIR, L // CHUNK, LANES), f32), big],
        compiler_params=_cparams(("arbitrary", "arbitrary")),
        name="dn_delta",
    )(qkv, qkv, qkv, cg, grow)


def _merge_kernel(x_ref, mod_ref, om_ref, zm_ref, od_ref, zd_ref, g_ref, wmo_ref, dnw_ref, wdo_ref, wout_ref, o_ref):
    ym = _dot(om_ref[0].astype(f32) * _silu(zm_ref[0].astype(f32)), wmo_ref[...])
    od = od_ref[0]
    lane = lax.broadcasted_iota(jnp.int32, (od.shape[0], LANES), 1)
    lo = lane < DK
    parts = []
    for j in range(od.shape[1] // LANES):
        y = od[:, j * LANES:(j + 1) * LANES]
        y2 = y * y
        s_lo = jnp.sum(jnp.where(lo, y2, 0.0), axis=-1, keepdims=True)
        s_hi = jnp.sum(jnp.where(lo, 0.0, y2), axis=-1, keepdims=True)
        parts.append(y * lax.rsqrt(jnp.where(lo, s_lo, s_hi) * (1.0 / DK) + EPS))
    odn = jnp.concatenate(parts, axis=1) * dnw_ref[...]
    yd = _dot(odn * _silu(zd_ref[0].astype(f32)), wdo_ref[...])
    D = ym.shape[1]
    g = g_ref[0].astype(f32)
    y = _dot(jax.nn.sigmoid(g[:, :D]) * ym + jax.nn.sigmoid(g[:, D:]) * yd, wout_ref[...])
    o_ref[0] = x_ref[0] + mod_ref[0, 2:3, :] * y


def _merge(x, mod, o_mla, zm, o_dn, zd, gates, w_mo, dn_nw, w_do, w_out, tm):
    B, T, D = x.shape
    W = o_mla.shape[-1]
    tok = lambda n: pl.BlockSpec((1, tm, n), lambda b, t: (b, t, 0))
    full = lambda a: pl.BlockSpec(a.shape, lambda b, t: (0, 0))
    return pl.pallas_call(
        _merge_kernel,
        grid=(B, T // tm),
        in_specs=[tok(D), pl.BlockSpec((1, 8, D), lambda b, t: (b, 0, 0)),
                  tok(W), tok(W), tok(W), tok(W), tok(2 * D),
                  full(w_mo), full(dn_nw), full(w_do), full(w_out)],
        out_specs=tok(D),
        out_shape=jax.ShapeDtypeStruct((B, T, D), f32),
        compiler_params=_cparams(("arbitrary", "arbitrary")),
        name="merge",
    )(x, mod, o_mla, zm, o_dn, zd, gates, w_mo, dn_nw, w_do, w_out)


def _rope_tables(T, q_scale):
    half = ROPE // 4
    axis_dims = ROPE // 2
    rows = T // GRID_W
    inv_freq = ROPE_THETA ** (-jnp.arange(0, axis_dims, 2, dtype=f32) / axis_dims)
    rowp = jnp.repeat(jnp.arange(rows, dtype=f32), GRID_W)[:, None] * inv_freq
    colp = jnp.tile(jnp.arange(GRID_W, dtype=f32), rows)[:, None] * inv_freq
    cos = jnp.concatenate([jnp.cos(rowp)] * 2 + [jnp.cos(colp)] * 2, axis=1)
    sin = jnp.concatenate([jnp.sin(rowp)] * 2 + [jnp.sin(colp)] * 2, axis=1)
    first = np.tile(np.concatenate([np.ones(half), np.zeros(half)]), 2).astype(np.float32)
    zl = jnp.zeros((T, LANES - QK), f32)
    c = jnp.concatenate([jnp.ones((T, NOPE), f32), cos, zl], axis=1)
    sp = jnp.concatenate([jnp.zeros((T, NOPE), f32), sin * (1.0 - first), zl], axis=1)
    sm = jnp.concatenate([jnp.zeros((T, NOPE), f32), -sin * first, zl], axis=1)
    return c, sp, sm, (c * q_scale, sp * q_scale, sm * q_scale)


def _pad_heads(w, per_head, keep):
    K = w.shape[0]
    wh = w.reshape(K, HEADS, per_head)[:, :, :keep]
    return jnp.pad(wh, ((0, 0), (0, 0), (0, LANES - keep))).reshape(K, HEADS * LANES)


def kernel(x, c, ctx, c_ctx, w_mod, b_mod, norm_w, w_in, mla_q_norm_w, mla_w_uq, mla_kv_norm_w, mla_w_ukv,
           mla_q_head_norm_w, mla_k_head_norm_w, mla_w_o, dn_conv_w, dn_a_log, dn_dt_bias, dn_out_norm_w, dn_w_o,
           w_out):
    B, T, D = x.shape
    LC = ctx.shape[1]
    assert w_mod.shape[0] == 1, "one layer"
    (w_mod, b_mod, norm_w, w_in, mla_q_norm_w, mla_w_uq, mla_kv_norm_w, mla_w_ukv, mla_q_head_norm_w,
     mla_k_head_norm_w, mla_w_o, dn_conv_w, dn_a_log, dn_dt_bias, dn_out_norm_w, dn_w_o, w_out) = (
        a[0] for a in (w_mod, b_mod, norm_w, w_in, mla_q_norm_w, mla_w_uq, mla_kv_norm_w, mla_w_ukv,
                       mla_q_head_norm_w, mla_k_head_norm_w, mla_w_o, dn_conv_w, dn_a_log, dn_dt_bias,
                       dn_out_norm_w, dn_w_o, w_out))
    QL, KVL = mla_w_uq.shape[0], mla_w_ukv.shape[0]
    WM, WK = HEADS * VD, HEADS * DK
    nh = N_DIR * HEADS

    R = -(-(B + 1) // 8) * 8
    cc = jnp.concatenate([c, c_ctx[None], jnp.zeros((R - B - 1, D), f32)], axis=0)
    mod = _modulation(cc, w_mod, b_mod).reshape(R, 3, D)
    mod8 = jnp.pad(mod, ((0, 0), (0, 5), (0, 0)))
    mod_lat = mod8[:B]
    mod_ctx = jnp.broadcast_to(mod8[B:B + 1], (B, 8, D))

    o = np.cumsum([0, QL, KVL, ROPE, WM, WK, WK, WK, WK, nh, nh, 2 * D])
    wcol = lambda i: w_in[:, o[i]:o[i + 1]]
    zeros = lambda n: jnp.zeros((D, n), f32)
    w_kr = jnp.concatenate([zeros(NOPE), wcol(2), zeros(LANES - QK)], axis=1)
    w_g = jnp.concatenate([wcol(8), wcol(9), zeros(LANES - 2 * nh)], axis=1)
    w_qkv = jnp.concatenate([wcol(4), wcol(5), wcol(6)], axis=1)
    w_lat = jnp.concatenate([wcol(0), wcol(1), w_kr, wcol(3), w_qkv, wcol(7), w_g, wcol(10)], axis=1).astype(bf16)
    w_ctx = jnp.concatenate([wcol(1), w_kr, w_qkv, w_g], axis=1).astype(bf16)
    cq, ckv_l, kr_l, zm, qkv_l, zd, g_l, gates = _inproj(
        x, mod_lat, norm_w, w_lat, [QL, KVL, LANES, WM, 3 * WK, WK, LANES, 2 * D],
        [bf16, bf16, f32, bf16, f32, bf16, f32, bf16], 512)
    ckv_c, kr_c, qkv_c, g_c = _inproj(ctx, mod_ctx, norm_w, w_ctx, [KVL, LANES, 3 * WK, LANES],
                                      [bf16, f32, f32, f32], LC)

    q_scale = (QK ** -0.5) * math.log2(math.e)
    c_t, sp_t, sm_t, q_tabs = _rope_tables(T, q_scale)
    wq = _pad_heads(mla_w_uq, QK, QK).astype(bf16)
    wk = _pad_heads(mla_w_ukv, NOPE + VD, NOPE).astype(bf16)
    wv = mla_w_ukv.reshape(KVL, HEADS, NOPE + VD)[:, :, NOPE:].reshape(KVL, WM).astype(bf16)
    hw_q = jnp.pad(mla_q_head_norm_w, (0, LANES - QK)).reshape(1, LANES)
    hw_k = jnp.pad(mla_k_head_norm_w, (0, LANES - QK)).reshape(1, LANES)
    q = _qprep(cq, mla_q_norm_w, wq, hw_q, q_tabs, 512)
    k_lat, v_lat = _kvprep(ckv_l, kr_l, mla_kv_norm_w, wk, wv, hw_k, (c_t, sp_t, sm_t), 512)
    k_ctx, v_ctx = _kvprep(ckv_c, kr_c, mla_kv_norm_w, wk, wv, hw_k, None, LC)
    o_mla = _attention(q, k_ctx, k_lat, v_ctx, v_lat, 512, 512)

    conv_w8 = jnp.pad(dn_conv_w, ((0, 8 - CONV_W), (0, 0)))
    qkv = _dn_conv(qkv_c, qkv_l, conv_w8, 256)
    a_lanes = jnp.pad(dn_a_log.reshape(1, nh), ((0, 0), (nh, LANES - 2 * nh)))
    dt_lanes = jnp.pad(dn_dt_bias.reshape(1, nh), ((0, 0), (nh, LANES - 2 * nh)))
    beta, gcum, egc, etg = _dn_gates(g_c, g_l, a_lanes, dt_lanes)
    L = LC + T
    pairs = HEADS // 2
    quant = jnp.stack([beta[..., :nh], gcum[..., nh:2 * nh], egc[..., nh:2 * nh], etg[..., nh:2 * nh]], axis=2)
    cg = quant.reshape(B, L, 4, N_DIR, pairs, 2).transpose(0, 4, 1, 3, 2, 5).reshape(B, pairs, L, N_DIR * 8)
    grow = gcum[..., nh:2 * nh].reshape(B, L // CHUNK, CHUNK, N_DIR, pairs, 2)
    grow = grow.transpose(0, 4, 3, 1, 5, 2).reshape(B, pairs, N_DIR, L // CHUNK, 2 * CHUNK)
    o_dn = _dn_delta(qkv, cg, grow, LC // CHUNK, 6)

    dn_nw = jnp.tile(dn_out_norm_w, HEADS).reshape(1, WK)
    return _merge(x, mod_lat, o_mla, zm, o_dn, zd, gates, mla_w_o.astype(bf16), dn_nw, dn_w_o.astype(bf16),
                  w_out.astype(bf16), 512)
```

```python
import functools
import math

import numpy as np
import jax
import jax.numpy as jnp
from jax import lax
from jax.experimental import pallas as pl
from jax.experimental.pallas import tpu as pltpu

f32 = jnp.float32
bf16 = jnp.bfloat16

HEADS = 8
NOPE = 64
ROPE = 32
QK = NOPE + ROPE
VD = 64
DK = 64
N_DIR = 2
CONV_W = 5
CHUNK = 64
GRID_W = 64
ROPE_THETA = 10000.0
EPS = 1e-6

LANES = 128
VMEM_LIMIT = 56 * 1024 * 1024

_NT = (((1,), (1,)), ((), ()))
_TN = (((0,), (0,)), ((), ()))


def _cparams(sem):
    return pltpu.CompilerParams(dimension_semantics=sem, vmem_limit_bytes=VMEM_LIMIT)


def _dot(a, b):
    return jnp.dot(a.astype(bf16), b.astype(bf16), preferred_element_type=f32)


def _silu(x):
    return x * jax.nn.sigmoid(x)


def _mod_kernel(c_ref, w_ref, b_ref, o_ref):
    o_ref[...] = jnp.dot(_silu(c_ref[...]), w_ref[...], preferred_element_type=f32,
                         precision=lax.Precision.HIGHEST) + b_ref[...]


def _modulation(cc, w_mod, b_mod):
    R, D = cc.shape
    N = w_mod.shape[1]
    nb = N // D
    return pl.pallas_call(
        _mod_kernel,
        grid=(nb,),
        in_specs=[pl.BlockSpec((R, D), lambda j: (0, 0)),
                  pl.BlockSpec((D, D), lambda j: (0, j)),
                  pl.BlockSpec((1, D), lambda j: (0, j))],
        out_specs=pl.BlockSpec((R, D), lambda j: (0, j)),
        out_shape=jax.ShapeDtypeStruct((R, N), f32),
        compiler_params=_cparams(("arbitrary",)),
        name="modulation",
    )(cc, w_mod, b_mod.reshape(1, N))


def _inproj_kernel(x_ref, mod_ref, nw_ref, w_ref, *out_refs, segs):
    x = x_ref[0]
    y = x * lax.rsqrt(jnp.mean(x * x, axis=-1, keepdims=True) + EPS) * nw_ref[...]
    h = (y * (1.0 + mod_ref[0, 1:2, :]) + mod_ref[0, 0:1, :]).astype(bf16)
    for (a, b), o_ref in zip(segs, out_refs):
        o_ref[0] = jnp.dot(h, w_ref[:, a:b], preferred_element_type=f32).astype(o_ref.dtype)


def _inproj(x, mod, norm_w, w, widths, dtypes, use, tm):
    B, L, D = x.shape
    offs = np.concatenate([[0], np.cumsum(widths)])
    segs = tuple((int(offs[i]), int(offs[i + 1])) for i in use)
    widths = [widths[i] for i in use]
    dtypes = [dtypes[i] for i in use]
    return pl.pallas_call(
        functools.partial(_inproj_kernel, segs=segs),
        grid=(B, L // tm),
        in_specs=[pl.BlockSpec((1, tm, D), lambda b, t: (b, t, 0)),
                  pl.BlockSpec((1, 8, D), lambda b, t: (b, 0, 0)),
                  pl.BlockSpec((1, D), lambda b, t: (0, 0)),
                  pl.BlockSpec(w.shape, lambda b, t: (0, 0))],
        out_specs=[pl.BlockSpec((1, tm, n), lambda b, t: (b, t, 0)) for n in widths],
        out_shape=[jax.ShapeDtypeStruct((B, L, n), dt) for n, dt in zip(widths, dtypes)],
        compiler_params=_cparams(("arbitrary", "arbitrary")),
        name="inproj",
    )(x, mod, norm_w.reshape(1, D), w)


def _head_norm_rope(x, hw, tabs):
    ss = jnp.sum(x * x, axis=-1, keepdims=True) * (1.0 / QK)
    xn = x * lax.rsqrt(ss + EPS) * hw
    if tabs is None:
        return xn
    c, sp, sm = tabs
    return xn * c + pltpu.roll(xn, 8, 1) * sp + pltpu.roll(xn, LANES - 8, 1) * sm


def _qprep_kernel(cq_ref, nw_ref, w_ref, hw_ref, c_ref, sp_ref, sm_ref, o_ref):
    cq = cq_ref[0].astype(f32)
    cn = cq * lax.rsqrt(jnp.mean(cq * cq, axis=-1, keepdims=True) + EPS) * nw_ref[...]
    qa = _dot(cn, w_ref[...])
    tabs = (c_ref[...], sp_ref[...], sm_ref[...])
    for h in range(HEADS):
        sl = slice(h * LANES, (h + 1) * LANES)
        o_ref[0, :, sl] = _head_norm_rope(qa[:, sl], hw_ref[...], tabs).astype(o_ref.dtype)


def _kvprep_kernel(ckv_ref, kr_ref, nw_ref, wk_ref, wv_ref, hw_ref, *rest, rope):
    if rope:
        c_ref, sp_ref, sm_ref, k_ref, v_ref = rest
        tabs = (c_ref[...], sp_ref[...], sm_ref[...])
    else:
        k_ref, v_ref = rest
        tabs = None
    ckv = ckv_ref[0].astype(f32)
    cn = (ckv * lax.rsqrt(jnp.mean(ckv * ckv, axis=-1, keepdims=True) + EPS) * nw_ref[...]).astype(bf16)
    v_ref[0] = jnp.dot(cn, wv_ref[...], preferred_element_type=f32).astype(v_ref.dtype)
    ka = jnp.dot(cn, wk_ref[...], preferred_element_type=f32)
    kr = kr_ref[0]
    for h in range(HEADS):
        sl = slice(h * LANES, (h + 1) * LANES)
        k_ref[0, :, sl] = _head_norm_rope(ka[:, sl] + kr, hw_ref[...], tabs).astype(k_ref.dtype)


def _qprep(cq, nw, wq, hw, tabs, tm):
    B, L, C = cq.shape
    N = wq.shape[1]
    tab_spec = pl.BlockSpec((tm, LANES), lambda b, t: (t, 0))
    return pl.pallas_call(
        _qprep_kernel,
        grid=(B, L // tm),
        in_specs=[pl.BlockSpec((1, tm, C), lambda b, t: (b, t, 0)),
                  pl.BlockSpec((1, C), lambda b, t: (0, 0)),
                  pl.BlockSpec(wq.shape, lambda b, t: (0, 0)),
                  pl.BlockSpec((1, LANES), lambda b, t: (0, 0)),
                  tab_spec, tab_spec, tab_spec],
        out_specs=pl.BlockSpec((1, tm, N), lambda b, t: (b, t, 0)),
        out_shape=jax.ShapeDtypeStruct((B, L, N), bf16),
        compiler_params=_cparams(("arbitrary", "arbitrary")),
        name="mla_q_prep",
    )(cq, nw.reshape(1, C), wq, hw, *tabs)


def _kvprep(ckv, kr, nw, wk, wv, hw, tabs, tm):
    B, L, C = ckv.shape
    rope = tabs is not None
    tab_spec = pl.BlockSpec((tm, LANES), lambda b, t: (t, 0))
    in_specs = [pl.BlockSpec((1, tm, C), lambda b, t: (b, t, 0)),
                pl.BlockSpec((1, tm, LANES), lambda b, t: (b, t, 0)),
                pl.BlockSpec((1, C), lambda b, t: (0, 0)),
                pl.BlockSpec(wk.shape, lambda b, t: (0, 0)),
                pl.BlockSpec(wv.shape, lambda b, t: (0, 0)),
                pl.BlockSpec((1, LANES), lambda b, t: (0, 0))]
    args = [ckv, kr, nw.reshape(1, C), wk, wv, hw]
    if rope:
        in_specs += [tab_spec] * 3
        args += list(tabs)
    return pl.pallas_call(
        functools.partial(_kvprep_kernel, rope=rope),
        grid=(B, L // tm),
        in_specs=in_specs,
        out_specs=[pl.BlockSpec((1, tm, wk.shape[1]), lambda b, t: (b, t, 0)),
                   pl.BlockSpec((1, tm, wv.shape[1]), lambda b, t: (b, t, 0))],
        out_shape=[jax.ShapeDtypeStruct((B, L, wk.shape[1]), bf16),
                   jax.ShapeDtypeStruct((B, L, wv.shape[1]), bf16)],
        compiler_params=_cparams(("arbitrary", "arbitrary")),
        name="mla_kv_prep_rope" if rope else "mla_kv_prep",
    )(*args)


def _attn_kernel(q_ref, kc_ref, kl_ref, vc_ref, vl_ref, o_ref, s_ref, *, kv_chunk):
    lc = kc_ref.shape[1]
    ll = kl_ref.shape[1]
    chunks = [(kc_ref, vc_ref, 0, lc, 0)]
    for off in range(0, ll, kv_chunk):
        chunks.append((kl_ref, vl_ref, off, kv_chunk, lc + off))
    nck = len(chunks)

    def scores(a, i, m):
        k_ref, _, off, n, col = chunks[i]
        sl = slice(a * LANES, (a + 1) * LANES)
        s = lax.dot_general(q_ref[0, :, sl], k_ref[0, off:off + n, sl], _NT, preferred_element_type=f32)
        s_ref[a, :, col:col + n] = s
        cm = jnp.max(s, axis=-1, keepdims=True)
        return cm if m is None else jnp.maximum(m, cm)

    def weighted(a, i, m, l, acc):
        _, v_ref, off, n, col = chunks[i]
        p = jnp.exp2(s_ref[a, :, col:col + n] - m)
        ps = jnp.sum(p, axis=-1, keepdims=True)
        pv = jnp.dot(p.astype(bf16), v_ref[0, off:off + n, :], preferred_element_type=f32)
        return (ps if l is None else l + ps), (pv if acc is None else acc + pv)

    m = [None, None]
    l = [None, None]
    acc = [None, None]
    for i in range(nck):
        m[0] = scores(0, i, m[0])
    for i in range(nck):
        l[0], acc[0] = weighted(0, i, m[0], l[0], acc[0])
        m[1] = scores(1, i, m[1])
    for i in range(nck):
        l[1], acc[1] = weighted(1, i, m[1], l[1], acc[1])
    lane = lax.broadcasted_iota(jnp.int32, acc[0].shape, 1)
    o_ref[0] = jnp.where(lane < VD, acc[0] / l[0], acc[1] / l[1]).astype(o_ref.dtype)


def _attention(q, k_ctx, k_lat, v_ctx, v_lat, tq, kv_chunk):
    B, T, _ = q.shape
    lc, ll = k_ctx.shape[1], k_lat.shape[1]
    pairs = HEADS // 2
    return pl.pallas_call(
        functools.partial(_attn_kernel, kv_chunk=kv_chunk),
        grid=(B, pairs, T // tq),
        in_specs=[pl.BlockSpec((1, tq, 2 * LANES), lambda b, p, t: (b, t, p)),
                  pl.BlockSpec((1, lc, 2 * LANES), lambda b, p, t: (b, 0, p)),
                  pl.BlockSpec((1, ll, 2 * LANES), lambda b, p, t: (b, 0, p)),
                  pl.BlockSpec((1, lc, LANES), lambda b, p, t: (b, 0, p)),
                  pl.BlockSpec((1, ll, LANES), lambda b, p, t: (b, 0, p))],
        out_specs=pl.BlockSpec((1, tq, LANES), lambda b, p, t: (b, t, p)),
        out_shape=jax.ShapeDtypeStruct((B, T, HEADS * VD), bf16),
        scratch_shapes=[pltpu.VMEM((2, tq, lc + ll), f32)],
        compiler_params=_cparams(("arbitrary", "arbitrary", "arbitrary")),
        name="mla_attention",
    )(q, k_ctx, k_lat, v_ctx, v_lat)


def _conv_kernel(xc_ref, xl_ref, w_ref, o_ref, pad_ref, *, tile):
    j = pl.program_id(1)
    nq = (HEADS * DK) // LANES
    lane = lax.broadcasted_iota(jnp.int32, (tile, LANES), 1)
    lo = lane < DK
    qscale = jnp.where(j < nq, DK ** -0.5, 1.0).astype(f32)
    is_qk = j < 2 * nq
    w = w_ref[...]
    half = CONV_W // 2
    row0 = 0
    for x_ref in (xc_ref, xl_ref):
        L = x_ref.shape[1]
        pad_ref[0:8, :] = jnp.zeros((8, LANES), f32)
        pad_ref[8:8 + L, :] = x_ref[0]
        pad_ref[8 + L:16 + L, :] = jnp.zeros((8, LANES), f32)
        for t0 in range(0, L, tile):
            y = None
            for tap in range(CONV_W):
                term = pad_ref[pl.ds(t0 + 8 - half + tap, tile), :] * w[tap:tap + 1, :]
                y = term if y is None else y + term
            y = _silu(y)
            y2 = y * y
            s_lo = jnp.sum(jnp.where(lo, y2, 0.0), axis=-1, keepdims=True)
            s_hi = jnp.sum(jnp.where(lo, 0.0, y2), axis=-1, keepdims=True)
            r = lax.rsqrt(jnp.where(lo, s_lo, s_hi) + EPS) * qscale
            o_ref[0, row0 + t0:row0 + t0 + tile, :] = y * jnp.where(is_qk, r, 1.0)
        row0 += L


def _dn_conv(x_ctx, x_lat, conv_w8, tile):
    B, lc, C = x_ctx.shape
    ll = x_lat.shape[1]
    return pl.pallas_call(
        functools.partial(_conv_kernel, tile=tile),
        grid=(B, C // LANES),
        in_specs=[pl.BlockSpec((1, lc, LANES), lambda b, j: (b, 0, j)),
                  pl.BlockSpec((1, ll, LANES), lambda b, j: (b, 0, j)),
                  pl.BlockSpec((8, LANES), lambda b, j: (0, j))],
        out_specs=pl.BlockSpec((1, lc + ll, LANES), lambda b, j: (b, 0, j)),
        out_shape=jax.ShapeDtypeStruct((B, lc + ll, C), f32),
        scratch_shapes=[pltpu.VMEM((max(lc, ll) + 16, LANES), f32)],
        compiler_params=_cparams(("arbitrary", "arbitrary")),
        name="dn_conv",
    )(x_ctx, x_lat, conv_w8)


def _gates_kernel(gc_ref, gl_ref, al_ref, dt_ref, beta_ref, gcum_ref, egc_ref, etg_ref):
    nh = N_DIR * HEADS
    row0 = 0
    for g_ref in (gc_ref, gl_ref):
        L = g_ref.shape[1]
        x = g_ref[0]
        rows = slice(row0, row0 + L)
        beta_ref[0, rows, :] = jax.nn.sigmoid(x)
        z = x + dt_ref[...]
        g = -jnp.exp(al_ref[...]) * (jnp.maximum(z, 0.0) + jnp.log1p(jnp.exp(-jnp.abs(z))))
        pos = lax.broadcasted_iota(jnp.int32, (L, LANES), 0) % CHUNK
        pre = g
        suf = g
        s = 1
        while s < CHUNK:
            pre = pre + jnp.where(pos >= s, pltpu.roll(pre, s, 0), 0.0)
            suf = suf + jnp.where(pos < CHUNK - s, pltpu.roll(suf, L - s, 0), 0.0)
            s *= 2
        lane = lax.broadcasted_iota(jnp.int32, (L, LANES), 1)
        fwd = lane < nh + HEADS
        gcum = jnp.where(fwd, pre, suf)
        tot = pre + suf - g
        gcum_ref[0, rows, :] = gcum
        egc_ref[0, rows, :] = jnp.exp(gcum)
        etg_ref[0, rows, :] = jnp.exp(tot - gcum)
        row0 += L


def _dn_gates(g_ctx, g_lat, a_lanes, dt_lanes):
    B, lc, _ = g_ctx.shape
    ll = g_lat.shape[1]
    L = lc + ll
    out = jax.ShapeDtypeStruct((B, L, LANES), f32)
    ospec = pl.BlockSpec((1, L, LANES), lambda b: (b, 0, 0))
    return pl.pallas_call(
        _gates_kernel,
        grid=(B,),
        in_specs=[pl.BlockSpec((1, lc, LANES), lambda b: (b, 0, 0)),
                  pl.BlockSpec((1, ll, LANES), lambda b: (b, 0, 0)),
                  pl.BlockSpec((1, LANES), lambda b: (0, 0)),
                  pl.BlockSpec((1, LANES), lambda b: (0, 0))],
        out_specs=[ospec] * 4,
        out_shape=[out] * 4,
        compiler_params=_cparams(("arbitrary",)),
        name="dn_gates",
    )(g_ctx, g_lat, a_lanes, dt_lanes)


def _dn_kernel(q_ref, k_ref, v_ref, cg_ref, gr_ref, o_ref, mt_s, nn_s, qt_s, et_s, o_s, *, n_ctx, unroll):
    C = CHUNK
    n_chunks = q_ref.shape[1] // C
    npair = q_ref.shape[2] // LANES
    row = lax.broadcasted_iota(jnp.int32, (C, LANES), 0)
    lane = lax.broadcasted_iota(jnp.int32, (C, LANES), 1)
    col = lane % C
    la = lane < C
    same16 = (row // 16) == (col // 16)
    eye = jnp.where(row == col, 1.0, 0.0).astype(f32)
    r2 = lax.broadcasted_iota(jnp.int32, (2 * C, LANES), 0)
    l2 = lax.broadcasted_iota(jnp.int32, (2 * C, LANES), 1)
    bd_mask = (r2 < C) == (l2 < C)
    rmask = jnp.concatenate([bd_mask, bd_mask], axis=1)

    def bd(y):
        m = bd_mask if y.shape[1] == LANES else rmask
        return jnp.where(m, jnp.concatenate([y, y], axis=0), 0.0).astype(bf16)

    def mm(x, y):
        return jnp.dot(x.astype(bf16), bd(y), preferred_element_type=f32)

    def precompute(g, carry):
        chains = []
        for i in range(unroll):
            c = g * unroll + i
            r0 = pl.multiple_of(c * C, C)
            for pp in range(npair):
                sl = slice(pp * LANES, (pp + 1) * LANES)
                q = q_ref[0, pl.ds(r0, C), sl]
                k = k_ref[0, pl.ds(r0, C), sl]
                v = v_ref[0, pl.ds(r0, C), sl]
                cg = cg_ref[0, pp, pl.ds(r0, C), :]
                kbd = bd(k)
                for d in range(N_DIR):
                    colb = lambda i, d=d, cg=cg: jnp.where(la, cg[:, d * 8 + i:d * 8 + i + 1],
                                                           cg[:, d * 8 + i + 1:d * 8 + i + 2])
                    chains.append(dict(c=c, r0=r0, d=d, pp=pp, q=q, k=k, v=v, kbd=kbd, beta=colb(0), gcol=colb(2),
                                       egc=colb(4), etg=colb(6), grow=gr_ref[0, pp, d, pl.ds(c, 1), :]))
        for s in chains:
            incl = (row >= col) if s["d"] == 0 else (row <= col)
            s["decay"] = jnp.where(incl, jnp.exp(jnp.where(incl, s["gcol"] - s["grow"], 0.0)), 0.0)
            s["kb"] = s["k"] * s["beta"]
        for s in chains:
            s["kkqk"] = lax.dot_general(jnp.concatenate([s["kb"], s["q"]], axis=0).astype(bf16), s["kbd"], _NT,
                                        preferred_element_type=f32)
        for s in chains:
            incl = (row >= col) if s["d"] == 0 else (row <= col)
            strict = (row > col) if s["d"] == 0 else (row < col)
            lm = jnp.where(strict, s["kkqk"][:C] * s["decay"], 0.0)
            s["qkm"] = jnp.where(incl, s["kkqk"][C:] * s["decay"], 0.0)
            s["dg"] = jnp.where(same16, lm, 0.0)
            s["e"] = lm - s["dg"]
        for s in chains:
            s["d2"] = mm(s["dg"], s["dg"])
        for s in chains:
            s["d4"] = mm(s["d2"], s["d2"])
            s["dd2"] = mm(s["dg"], s["d2"])
        for s in chains:
            s["d8"] = mm(s["d4"], s["d4"])
        for s in chains:
            s["d48"] = mm(s["d4"], s["d8"])
        for s in chains:
            x1 = eye - s["dg"] + s["d2"] - s["dd2"]
            x2 = eye + s["d4"] + s["d8"] + s["d48"]
            s["t16"] = mm(x1, x2)
        for s in chains:
            s["n1"] = mm(s["t16"], s["e"])
        for s in chains:
            s["n2"] = mm(s["n1"], s["n1"])
        for s in chains:
            s["n3"] = mm(s["n1"], s["n2"])
        for s in chains:
            s["tinv"] = mm(eye - s["n1"] + s["n2"] - s["n3"], s["t16"])
        for s in chains:
            rhs = jnp.concatenate([s["v"] * s["beta"], s["kb"] * s["egc"]], axis=1)
            s["uw"] = mm(s["tinv"], rhs)
        for s in chains:
            s["nm"] = lax.dot_general(s["uw"].astype(bf16), (s["k"] * s["etg"]).astype(bf16), _TN,
                                      preferred_element_type=f32)
            s["oq"] = mm(s["qkm"], s["uw"])
        for s in chains:
            d, pp, r0 = s["d"], s["pp"], s["r0"]
            r2c = pl.multiple_of(s["c"] * 2 * C, 2 * C)
            last = C - 1 if d == 0 else 0
            nn_s[d, pp, pl.ds(r2c, 2 * C), :] = jnp.where(bd_mask, s["nm"][:LANES], 0.0)
            mt_s[d, pp, pl.ds(r2c, 2 * C), :] = jnp.where(bd_mask, s["nm"][LANES:], 0.0).astype(bf16)
            o_s[d, pp, pl.ds(r0, C), :] = s["oq"][:, :LANES]
            qt_s[d, pp, pl.ds(r0, C), :] = (s["q"] * s["egc"] - s["oq"][:, LANES:]).astype(bf16)
            et_s[d, pp, pl.ds(s["c"], 1), :] = s["egc"][last:last + 1, :]
        return carry

    lax.fori_loop(0, n_chunks // unroll, precompute, 0)

    idx = [(d, pp) for d in range(N_DIR) for pp in range(npair)]

    def scan(n, states):
        cs = [n, jnp.where(n < n_ctx, n_ctx - 1 - n, n_chunks + n_ctx - 1 - n)]
        r1 = [pl.multiple_of(c * C, C) for c in cs]
        r2c = [pl.multiple_of(c * 2 * C, 2 * C) for c in cs]
        stb = [st.astype(bf16) for st in states]
        prod = [jnp.dot(stb[j], mt_s[d, pp, pl.ds(r2c[d], 2 * C), :], preferred_element_type=f32)
                for j, (d, pp) in enumerate(idx)]
        new = tuple(states[j] * et_s[d, pp, pl.ds(cs[d], 1), :] - prod[j] + nn_s[d, pp, pl.ds(r2c[d], 2 * C), :]
                    for j, (d, pp) in enumerate(idx))
        for j, (d, pp) in enumerate(idx):
            o_s[d, pp, pl.ds(r1[d], C), :] = o_s[d, pp, pl.ds(r1[d], C), :] + lax.dot_general(
                qt_s[d, pp, pl.ds(r1[d], C), :], stb[j], _NT, preferred_element_type=f32)
        return new

    zero = jnp.zeros((2 * C, LANES), f32)
    lax.fori_loop(0, n_chunks, scan, tuple(zero for _ in idx))
    for pp in range(npair):
        o_ref[0, :, pp * LANES:(pp + 1) * LANES] = o_s[0, pp, n_ctx * C:, :] + o_s[1, pp, n_ctx * C:, :]


def _dn_delta(qkv, cg, grow, n_ctx, unroll, npair):
    B, L, _ = qkv.shape
    pairs = HEADS // 2
    ll = L - n_ctx * CHUNK
    nb = (HEADS * DK) // (npair * LANES)
    n_chunks = L // CHUNK
    w = npair * LANES
    return pl.pallas_call(
        functools.partial(_dn_kernel, n_ctx=n_ctx, unroll=unroll),
        grid=(B, pairs // npair),
        in_specs=[pl.BlockSpec((1, L, w), lambda b, p: (b, 0, p)),
                  pl.BlockSpec((1, L, w), lambda b, p: (b, 0, nb + p)),
                  pl.BlockSpec((1, L, w), lambda b, p: (b, 0, 2 * nb + p)),
                  pl.BlockSpec((1, npair, L, cg.shape[-1]), lambda b, p: (b, p, 0, 0)),
                  pl.BlockSpec((1, npair, N_DIR, n_chunks, LANES), lambda b, p: (b, p, 0, 0, 0))],
        out_specs=pl.BlockSpec((1, ll, w), lambda b, p: (b, 0, p)),
        out_shape=jax.ShapeDtypeStruct((B, ll, HEADS * DK), f32),
        scratch_shapes=[pltpu.VMEM((N_DIR, npair, 2 * L, LANES), bf16),
                        pltpu.VMEM((N_DIR, npair, 2 * L, LANES), f32),
                        pltpu.VMEM((N_DIR, npair, L, LANES), bf16),
                        pltpu.VMEM((N_DIR, npair, n_chunks, LANES), f32),
                        pltpu.VMEM((N_DIR, npair, L, LANES), f32)],
        compiler_params=_cparams(("arbitrary", "arbitrary")),
        name="dn_delta",
    )(qkv, qkv, qkv, cg, grow)


def _merge_kernel(x_ref, mod_ref, om_ref, zm_ref, od_ref, zd_ref, g_ref, wmo_ref, dnw_ref, wdo_ref, wout_ref, o_ref):
    ym = _dot(om_ref[0].astype(f32) * _silu(zm_ref[0].astype(f32)), wmo_ref[...])
    od = od_ref[0]
    lane = lax.broadcasted_iota(jnp.int32, (od.shape[0], LANES), 1)
    lo = lane < DK
    parts = []
    for j in range(od.shape[1] // LANES):
        y = od[:, j * LANES:(j + 1) * LANES]
        y2 = y * y
        s_lo = jnp.sum(jnp.where(lo, y2, 0.0), axis=-1, keepdims=True)
        s_hi = jnp.sum(jnp.where(lo, 0.0, y2), axis=-1, keepdims=True)
        parts.append(y * lax.rsqrt(jnp.where(lo, s_lo, s_hi) * (1.0 / DK) + EPS))
    odn = jnp.concatenate(parts, axis=1) * dnw_ref[...]
    yd = _dot(odn * _silu(zd_ref[0].astype(f32)), wdo_ref[...])
    D = ym.shape[1]
    g = g_ref[0].astype(f32)
    y = _dot(jax.nn.sigmoid(g[:, :D]) * ym + jax.nn.sigmoid(g[:, D:]) * yd, wout_ref[...])
    o_ref[0] = x_ref[0] + mod_ref[0, 2:3, :] * y


def _merge(x, mod, o_mla, zm, o_dn, zd, gates, w_mo, dn_nw, w_do, w_out, tm):
    B, T, D = x.shape
    W = o_mla.shape[-1]
    tok = lambda n: pl.BlockSpec((1, tm, n), lambda b, t: (b, t, 0))
    full = lambda a: pl.BlockSpec(a.shape, lambda b, t: (0, 0))
    return pl.pallas_call(
        _merge_kernel,
        grid=(B, T // tm),
        in_specs=[tok(D), pl.BlockSpec((1, 8, D), lambda b, t: (b, 0, 0)),
                  tok(W), tok(W), tok(W), tok(W), tok(2 * D),
                  full(w_mo), full(dn_nw), full(w_do), full(w_out)],
        out_specs=tok(D),
        out_shape=jax.ShapeDtypeStruct((B, T, D), f32),
        compiler_params=_cparams(("arbitrary", "arbitrary")),
        name="merge",
    )(x, mod, o_mla, zm, o_dn, zd, gates, w_mo, dn_nw, w_do, w_out)


def _rope_tables(T, q_scale):
    half = ROPE // 4
    axis_dims = ROPE // 2
    rows = T // GRID_W
    inv_freq = ROPE_THETA ** (-jnp.arange(0, axis_dims, 2, dtype=f32) / axis_dims)
    rowp = jnp.repeat(jnp.arange(rows, dtype=f32), GRID_W)[:, None] * inv_freq
    colp = jnp.tile(jnp.arange(GRID_W, dtype=f32), rows)[:, None] * inv_freq
    cos = jnp.concatenate([jnp.cos(rowp)] * 2 + [jnp.cos(colp)] * 2, axis=1)
    sin = jnp.concatenate([jnp.sin(rowp)] * 2 + [jnp.sin(colp)] * 2, axis=1)
    first = np.tile(np.concatenate([np.ones(half), np.zeros(half)]), 2).astype(np.float32)
    zl = jnp.zeros((T, LANES - QK), f32)
    c = jnp.concatenate([jnp.ones((T, NOPE), f32), cos, zl], axis=1)
    sp = jnp.concatenate([jnp.zeros((T, NOPE), f32), sin * (1.0 - first), zl], axis=1)
    sm = jnp.concatenate([jnp.zeros((T, NOPE), f32), -sin * first, zl], axis=1)
    return c, sp, sm, (c * q_scale, sp * q_scale, sm * q_scale)


def _pad_heads(w, per_head, keep):
    K = w.shape[0]
    wh = w.reshape(K, HEADS, per_head)[:, :, :keep]
    return jnp.pad(wh, ((0, 0), (0, 0), (0, LANES - keep))).reshape(K, HEADS * LANES)


def kernel(x, c, ctx, c_ctx, w_mod, b_mod, norm_w, w_in, mla_q_norm_w, mla_w_uq, mla_kv_norm_w, mla_w_ukv,
           mla_q_head_norm_w, mla_k_head_norm_w, mla_w_o, dn_conv_w, dn_a_log, dn_dt_bias, dn_out_norm_w, dn_w_o,
           w_out):
    B, T, D = x.shape
    LC = ctx.shape[1]
    assert w_mod.shape[0] == 1, "one layer"
    (w_mod, b_mod, norm_w, w_in, mla_q_norm_w, mla_w_uq, mla_kv_norm_w, mla_w_ukv, mla_q_head_norm_w,
     mla_k_head_norm_w, mla_w_o, dn_conv_w, dn_a_log, dn_dt_bias, dn_out_norm_w, dn_w_o, w_out) = (
        a[0] for a in (w_mod, b_mod, norm_w, w_in, mla_q_norm_w, mla_w_uq, mla_kv_norm_w, mla_w_ukv,
                       mla_q_head_norm_w, mla_k_head_norm_w, mla_w_o, dn_conv_w, dn_a_log, dn_dt_bias,
                       dn_out_norm_w, dn_w_o, w_out))
    QL, KVL = mla_w_uq.shape[0], mla_w_ukv.shape[0]
    WM, WK = HEADS * VD, HEADS * DK
    nh = N_DIR * HEADS

    R = -(-(B + 1) // 8) * 8
    cc = jnp.concatenate([c, c_ctx[None], jnp.zeros((R - B - 1, D), f32)], axis=0)
    mod = _modulation(cc, w_mod, b_mod).reshape(R, 3, D)
    mod8 = jnp.pad(mod, ((0, 0), (0, 5), (0, 0)))
    mod_lat = mod8[:B]
    mod_ctx = jnp.broadcast_to(mod8[B:B + 1], (B, 8, D))

    o = np.cumsum([0, QL, KVL, ROPE, WM, WK, WK, WK, WK, nh, nh, 2 * D])
    wcol = lambda i: w_in[:, o[i]:o[i + 1]]
    zeros = lambda n: jnp.zeros((D, n), f32)
    w_kr = jnp.concatenate([zeros(NOPE), wcol(2), zeros(LANES - QK)], axis=1)
    w_g = jnp.concatenate([wcol(8), wcol(9), zeros(LANES - 2 * nh)], axis=1)
    w_qkv = jnp.concatenate([wcol(4), wcol(5), wcol(6)], axis=1)
    w_all = jnp.concatenate([wcol(0), wcol(1), w_kr, wcol(3), w_qkv, wcol(7), w_g, wcol(10)], axis=1).astype(bf16)
    widths = [QL, KVL, LANES, WM, 3 * WK, WK, LANES, 2 * D]
    dtypes = [bf16, bf16, f32, bf16, f32, bf16, f32, bf16]
    cq, ckv_l, kr_l, zm, qkv_l, zd, g_l, gates = _inproj(x, mod_lat, norm_w, w_all, widths, dtypes, range(8), 512)
    ckv_c, kr_c, qkv_c, g_c = _inproj(ctx, mod_ctx, norm_w, w_all, widths, dtypes, (1, 2, 4, 6), LC)

    q_scale = (QK ** -0.5) * math.log2(math.e)
    c_t, sp_t, sm_t, q_tabs = _rope_tables(T, q_scale)
    wq = _pad_heads(mla_w_uq, QK, QK).astype(bf16)
    wk = _pad_heads(mla_w_ukv, NOPE + VD, NOPE).astype(bf16)
    wv = mla_w_ukv.reshape(KVL, HEADS, NOPE + VD)[:, :, NOPE:].reshape(KVL, WM).astype(bf16)
    hw_q = jnp.pad(mla_q_head_norm_w, (0, LANES - QK)).reshape(1, LANES)
    hw_k = jnp.pad(mla_k_head_norm_w, (0, LANES - QK)).reshape(1, LANES)
    q = _qprep(cq, mla_q_norm_w, wq, hw_q, q_tabs, 512)
    k_lat, v_lat = _kvprep(ckv_l, kr_l, mla_kv_norm_w, wk, wv, hw_k, (c_t, sp_t, sm_t), 512)
    k_ctx, v_ctx = _kvprep(ckv_c, kr_c, mla_kv_norm_w, wk, wv, hw_k, None, LC)
    o_mla = _attention(q, k_ctx, k_lat, v_ctx, v_lat, 512, 512)

    conv_w8 = jnp.pad(dn_conv_w, ((0, 8 - CONV_W), (0, 0)))
    qkv = _dn_conv(qkv_c, qkv_l, conv_w8, 256)
    a_lanes = jnp.pad(dn_a_log.reshape(1, nh), ((0, 0), (nh, LANES - 2 * nh)))
    dt_lanes = jnp.pad(dn_dt_bias.reshape(1, nh), ((0, 0), (nh, LANES - 2 * nh)))
    beta, gcum, egc, etg = _dn_gates(g_c, g_l, a_lanes, dt_lanes)
    L = LC + T
    pairs = HEADS // 2
    quant = jnp.stack([beta[..., :nh], gcum[..., nh:2 * nh], egc[..., nh:2 * nh], etg[..., nh:2 * nh]], axis=2)
    cg = quant.reshape(B, L, 4, N_DIR, pairs, 2).transpose(0, 4, 1, 3, 2, 5).reshape(B, pairs, L, N_DIR * 8)
    grow = gcum[..., nh:2 * nh].reshape(B, L // CHUNK, CHUNK, N_DIR, pairs, 2)
    grow = grow.transpose(0, 4, 3, 1, 5, 2).reshape(B, pairs, N_DIR, L // CHUNK, 2 * CHUNK)
    o_dn = _dn_delta(qkv, cg, grow, LC // CHUNK, 6, 2)

    dn_nw = jnp.tile(dn_out_norm_w, HEADS).reshape(1, WK)
    return _merge(x, mod_lat, o_mla, zm, o_dn, zd, gates, mla_w_o.astype(bf16), dn_nw, dn_w_o.astype(bf16),
                  w_out.astype(bf16), 512)
```

```python
import functools
import math

import numpy as np
import jax
import jax.numpy as jnp
from jax import lax
from jax.experimental import pallas as pl
from jax.experimental.pallas import tpu as pltpu

f32 = jnp.float32
bf16 = jnp.bfloat16

HEADS = 8
NOPE = 64
ROPE = 32
QK = NOPE + ROPE
VD = 64
DK = 64
N_DIR = 2
CONV_W = 5
CHUNK = 64
GRID_W = 64
ROPE_THETA = 10000.0
EPS = 1e-6

LANES = 128
VMEM_LIMIT = 56 * 1024 * 1024
PAIRS = HEADS // 2
PAIRS_PER_STEP = 2
GATE_GROUP = 8

_NT = (((1,), (1,)), ((), ()))
_TN = (((0,), (0,)), ((), ()))


def _cparams(sem):
    return pltpu.CompilerParams(dimension_semantics=sem, vmem_limit_bytes=VMEM_LIMIT)


def _dot(a, b):
    return jnp.dot(a.astype(bf16), b.astype(bf16), preferred_element_type=f32)


def _silu(x):
    return x * jax.nn.sigmoid(x)


def _rms(x):
    return x * lax.rsqrt(jnp.mean(x * x, axis=-1, keepdims=True) + EPS)


def _const_spec(shape):
    nd = len(shape)
    return pl.BlockSpec(shape, lambda *_: (0,) * nd, pipeline_mode=pl.Buffered(1))


def _mod_kernel(c_ref, w_ref, b_ref, o_ref):
    o_ref[...] = jnp.dot(_silu(c_ref[...]), w_ref[...], preferred_element_type=f32,
                         precision=lax.Precision.HIGHEST) + b_ref[...]


def _modulation(cc, w_mod, b_mod):
    R, D = cc.shape
    N = w_mod.shape[1]
    nb = N // D
    return pl.pallas_call(
        _mod_kernel,
        grid=(nb,),
        in_specs=[pl.BlockSpec((R, D), lambda j: (0, 0)),
                  pl.BlockSpec((D, D), lambda j: (0, j)),
                  pl.BlockSpec((1, D), lambda j: (0, j))],
        out_specs=pl.BlockSpec((R, D), lambda j: (0, j)),
        out_shape=jax.ShapeDtypeStruct((R, N), f32),
        compiler_params=_cparams(("arbitrary",)),
        name="modulation",
    )(cc, w_mod, b_mod.reshape(1, N))


def _proj_kernel(*refs, lead, segs, with_q, rope, q_scale):
    it = iter(refs)
    x_ref, mod_ref, nw_ref, w_ref, kvnw_ref, wk_ref, wv_ref, hwk_ref = (next(it) for _ in range(8))
    if with_q:
        qnw_ref, wq_ref, hwq_ref = (next(it) for _ in range(3))
    if rope:
        cos_ref, sin_ref = next(it), next(it)
    if with_q:
        q_ref = next(it)
    k_ref, v_ref = next(it), next(it)
    out_refs = list(it)

    nk = HEADS * LANES
    h = (_rms(x_ref[0]) * nw_ref[...] * (1.0 + mod_ref[0, 1:2, :]) + mod_ref[0, 0:1, :]).astype(bf16)
    part = lambda ab: jnp.dot(h, w_ref[:, ab[0]:ab[1]], preferred_element_type=f32)
    if with_q:
        cqn = (_rms(part(lead[0])) * qnw_ref[...]).astype(bf16)
    ckvn = (_rms(part(lead[1])) * kvnw_ref[...]).astype(bf16)
    kr = part(lead[2])
    if with_q:
        qq = jnp.dot(cqn, wq_ref[...], preferred_element_type=f32)
    ka = jnp.dot(ckvn, wk_ref[...], preferred_element_type=f32)
    v_ref[0] = jnp.dot(ckvn, wv_ref[...], preferred_element_type=f32).astype(v_ref.dtype)
    for ab, o_ref in zip(segs, out_refs):
        o_ref[0] = part(ab).astype(o_ref.dtype)

    if rope:
        cos, sin = cos_ref[...], sin_ref[...]
        lane = lax.broadcasted_iota(jnp.int32, kr.shape, 1)
        krs = jnp.where((lane & 8) != 0, pltpu.roll(kr, 8, 1), pltpu.roll(kr, LANES - 8, 1))
        ka_c, ka_s = hwk_ref[0:1, :] * cos, hwk_ref[1:2, :] * sin
        if with_q:
            qa_c, qa_s = (hwq_ref[0:1, :] * q_scale) * cos, (hwq_ref[1:2, :] * q_scale) * sin
    for hd in range(HEADS):
        sl = slice(hd * LANES, (hd + 1) * LANES)
        xk = ka[:, sl] + kr
        rk = lax.rsqrt(jnp.sum(xk * xk, axis=-1, keepdims=True) * (1.0 / QK) + EPS)
        if rope:
            k_ref[0, :, sl] = ((xk * ka_c + krs * ka_s) * rk).astype(k_ref.dtype)
        else:
            k_ref[0, :, sl] = (xk * hwk_ref[0:1, :] * rk).astype(k_ref.dtype)
        if with_q:
            xq = qq[:, sl]
            rq = lax.rsqrt(jnp.sum(xq * xq, axis=-1, keepdims=True) * (1.0 / QK) + EPS)
            q_ref[0, :, sl] = ((xq * qa_c + qq[:, nk + hd * LANES:nk + (hd + 1) * LANES] * qa_s) * rq
                               ).astype(q_ref.dtype)


def _project(x, mod, norm_w, w, widths, dtypes, use, kv, q, tabs, q_scale, tm):
    B, L, D = x.shape
    offs = np.concatenate([[0], np.cumsum(widths)])
    rng = lambda i: (int(offs[i]), int(offs[i + 1]))
    tok = lambda n: pl.BlockSpec((1, tm, n), lambda b, t: (b, t, 0))
    args = [x, mod, norm_w.reshape(1, D), w, *kv]
    in_specs = [tok(D), pl.BlockSpec((1, 8, D), lambda b, t: (b, 0, 0)), _const_spec((1, D)), _const_spec(w.shape)]
    in_specs += [_const_spec(a.shape) for a in kv]
    if q is not None:
        args += list(q)
        in_specs += [_const_spec(a.shape) for a in q]
    if tabs is not None:
        args += list(tabs)
        in_specs += [pl.BlockSpec((tm, LANES), lambda b, t: (t, 0))] * 2
    nk = HEADS * LANES
    out_widths = ([nk] if q is not None else []) + [nk, HEADS * VD] + [widths[i] for i in use]
    out_dtypes = ([bf16] if q is not None else []) + [bf16, bf16] + [dtypes[i] for i in use]
    return pl.pallas_call(
        functools.partial(_proj_kernel, lead=(rng(0), rng(1), rng(2)), segs=tuple(rng(i) for i in use),
                          with_q=q is not None, rope=tabs is not None, q_scale=q_scale),
        grid=(B, L // tm),
        in_specs=in_specs,
        out_specs=[tok(n) for n in out_widths],
        out_shape=[jax.ShapeDtypeStruct((B, L, n), dt) for n, dt in zip(out_widths, out_dtypes)],
        compiler_params=_cparams(("arbitrary", "arbitrary")),
        name="project_lat" if q is not None else "project_ctx",
    )(*args)


def _attn_kernel(q_ref, kc_ref, kl_ref, vc_ref, vl_ref, o_ref, s_ref, *, kv_chunk, rows):
    lc = kc_ref.shape[1]
    ll = kl_ref.shape[1]
    chunks = [(kc_ref, vc_ref, 0, lc, 0)]
    for off in range(0, ll, kv_chunk):
        chunks.append((kl_ref, vl_ref, off, kv_chunk, lc + off))
    nck = len(chunks)
    streams = [(r0, a) for r0 in range(0, q_ref.shape[1], rows) for a in range(2)]

    def scores(j, i, m):
        r0, a = streams[j]
        k_ref, _, off, n, col = chunks[i]
        sl = slice(a * LANES, (a + 1) * LANES)
        s = lax.dot_general(q_ref[0, r0:r0 + rows, sl], k_ref[0, off:off + n, sl], _NT, preferred_element_type=f32)
        s_ref[j % 2, :, col:col + n] = s
        cm = jnp.max(s, axis=-1, keepdims=True)
        return cm if m is None else jnp.maximum(m, cm)

    def weighted(j, i, m, l, acc):
        _, v_ref, off, n, col = chunks[i]
        p = jnp.exp2(s_ref[j % 2, :, col:col + n] - m)
        ps = jnp.sum(p, axis=-1, keepdims=True)
        pv = jnp.dot(p.astype(bf16), v_ref[0, off:off + n, :], preferred_element_type=f32)
        return (ps if l is None else l + ps), (pv if acc is None else acc + pv)

    ns = len(streams)
    m = [None] * ns
    res = [None] * ns
    for j in range(ns + 1):
        l, acc = None, None
        for i in range(nck):
            if j > 0:
                l, acc = weighted(j - 1, i, m[j - 1], l, acc)
            if j < ns:
                m[j] = scores(j, i, m[j])
        if j > 0:
            res[j - 1] = acc / l
    lane = lax.broadcasted_iota(jnp.int32, res[0].shape, 1)
    for j in range(0, ns, 2):
        r0 = streams[j][0]
        o_ref[0, r0:r0 + rows, :] = jnp.where(lane < VD, res[j], res[j + 1]).astype(o_ref.dtype)


def _attention(q, k_ctx, k_lat, v_ctx, v_lat, tq, rows, kv_chunk):
    B, T, _ = q.shape
    lc, ll = k_ctx.shape[1], k_lat.shape[1]
    return pl.pallas_call(
        functools.partial(_attn_kernel, kv_chunk=kv_chunk, rows=rows),
        grid=(B, PAIRS, T // tq),
        in_specs=[pl.BlockSpec((1, tq, 2 * LANES), lambda b, p, t: (b, t, p)),
                  pl.BlockSpec((1, lc, 2 * LANES), lambda b, p, t: (b, 0, p)),
                  pl.BlockSpec((1, ll, 2 * LANES), lambda b, p, t: (b, 0, p)),
                  pl.BlockSpec((1, lc, LANES), lambda b, p, t: (b, 0, p)),
                  pl.BlockSpec((1, ll, LANES), lambda b, p, t: (b, 0, p))],
        out_specs=pl.BlockSpec((1, tq, LANES), lambda b, p, t: (b, t, p)),
        out_shape=jax.ShapeDtypeStruct((B, T, HEADS * VD), bf16),
        scratch_shapes=[pltpu.VMEM((2, rows, lc + ll), f32)],
        compiler_params=_cparams(("arbitrary", "arbitrary", "arbitrary")),
        name="mla_attention",
    )(q, k_ctx, k_lat, v_ctx, v_lat)


def _conv_kernel(xc_ref, xl_ref, w_ref, o_ref, pad_ref, *, tile):
    j = pl.program_id(1)
    nq = (HEADS * DK) // LANES
    lane = lax.broadcasted_iota(jnp.int32, (tile, LANES), 1)
    lo = lane < DK
    qscale = jnp.where(j < nq, DK ** -0.5, 1.0).astype(f32)
    is_qk = j < 2 * nq
    w = w_ref[...]
    half = CONV_W // 2
    row0 = 0
    for x_ref in (xc_ref, xl_ref):
        L = x_ref.shape[1]
        pad_ref[0:8, :] = jnp.zeros((8, LANES), f32)
        pad_ref[8:8 + L, :] = x_ref[0]
        pad_ref[8 + L:16 + L, :] = jnp.zeros((8, LANES), f32)
        for t0 in range(0, L, tile):
            y = None
            for tap in range(CONV_W):
                term = pad_ref[pl.ds(t0 + 8 - half + tap, tile), :] * w[tap:tap + 1, :]
                y = term if y is None else y + term
            y = _silu(y)
            y2 = y * y
            s_lo = jnp.sum(jnp.where(lo, y2, 0.0), axis=-1, keepdims=True)
            s_hi = jnp.sum(jnp.where(lo, 0.0, y2), axis=-1, keepdims=True)
            r = lax.rsqrt(jnp.where(lo, s_lo, s_hi) + EPS) * qscale
            o_ref[0, row0 + t0:row0 + t0 + tile, :] = y * jnp.where(is_qk, r, 1.0)
        row0 += L


def _dn_conv(x_ctx, x_lat, conv_w8, tile):
    B, lc, C = x_ctx.shape
    ll = x_lat.shape[1]
    return pl.pallas_call(
        functools.partial(_conv_kernel, tile=tile),
        grid=(B, C // LANES),
        in_specs=[pl.BlockSpec((1, lc, LANES), lambda b, j: (b, 0, j)),
                  pl.BlockSpec((1, ll, LANES), lambda b, j: (b, 0, j)),
                  pl.BlockSpec((8, LANES), lambda b, j: (0, j))],
        out_specs=pl.BlockSpec((1, lc + ll, LANES), lambda b, j: (b, 0, j)),
        out_shape=jax.ShapeDtypeStruct((B, lc + ll, C), f32),
        scratch_shapes=[pltpu.VMEM((max(lc, ll) + 16, LANES), f32)],
        compiler_params=_cparams(("arbitrary", "arbitrary")),
        name="dn_conv",
    )(x_ctx, x_lat, conv_w8)


def _gates_kernel(gc_ref, gl_ref, al_ref, dt_ref, o_ref):
    row0 = 0
    for g_ref in (gc_ref, gl_ref):
        L, W = g_ref.shape[1], g_ref.shape[2]
        x = g_ref[0]
        z = x + dt_ref[...]
        g = -jnp.exp(al_ref[...]) * (jnp.maximum(z, 0.0) + jnp.log1p(jnp.exp(-jnp.abs(z))))
        pos = lax.broadcasted_iota(jnp.int32, (L, W), 0) % CHUNK
        pre = g
        suf = g
        s = 1
        while s < CHUNK:
            pre = pre + jnp.where(pos >= s, pltpu.roll(pre, s, 0), 0.0)
            suf = suf + jnp.where(pos < CHUNK - s, pltpu.roll(suf, L - s, 0), 0.0)
            s *= 2
        lane = lax.broadcasted_iota(jnp.int32, (L, W), 1) % LANES
        fwd = (lane % GATE_GROUP) < GATE_GROUP // N_DIR
        gcum = jnp.where(fwd, pre, suf)
        tot = pre + suf - g
        grp = lane // GATE_GROUP
        o_ref[0, row0:row0 + L, :] = jnp.where(grp == 0, jax.nn.sigmoid(x),
                                               jnp.where(grp == 1, gcum,
                                                         jnp.where(grp == 2, jnp.exp(gcum), jnp.exp(tot - gcum))))
        row0 += L


def _dn_gates(g_ctx, g_lat, a_lanes, dt_lanes):
    B, lc, W = g_ctx.shape
    ll = g_lat.shape[1]
    L = lc + ll
    return pl.pallas_call(
        _gates_kernel,
        grid=(B,),
        in_specs=[pl.BlockSpec((1, lc, W), lambda b: (b, 0, 0)),
                  pl.BlockSpec((1, ll, W), lambda b: (b, 0, 0)),
                  pl.BlockSpec((1, W), lambda b: (0, 0)),
                  pl.BlockSpec((1, W), lambda b: (0, 0))],
        out_specs=pl.BlockSpec((1, L, W), lambda b: (b, 0, 0)),
        out_shape=jax.ShapeDtypeStruct((B, L, W), f32),
        compiler_params=_cparams(("arbitrary",)),
        name="dn_gates",
    )(g_ctx, g_lat, a_lanes, dt_lanes)


def _dn_kernel(q_ref, k_ref, v_ref, cg_ref, o_ref, mt_s, nn_s, qt_s, et_s, o_s, *, n_ctx, unroll):
    C = CHUNK
    n_chunks = q_ref.shape[1] // C
    npair = q_ref.shape[2] // LANES
    row = lax.broadcasted_iota(jnp.int32, (C, LANES), 0)
    lane = lax.broadcasted_iota(jnp.int32, (C, LANES), 1)
    col = lane % C
    la = lane < C
    same16 = (row // 16) == (col // 16)
    diag = row == col
    eye = jnp.where(diag, 1.0, 0.0).astype(f32)
    r2 = lax.broadcasted_iota(jnp.int32, (2 * C, LANES), 0)
    l2 = lax.broadcasted_iota(jnp.int32, (2 * C, LANES), 1)
    bd_mask = (r2 < C) == (l2 < C)
    rmask = jnp.concatenate([bd_mask, bd_mask], axis=1)

    def bd(y):
        m = bd_mask if y.shape[1] == LANES else rmask
        return jnp.where(m, jnp.concatenate([y, y], axis=0), 0.0).astype(bf16)

    def mm(x, y):
        return jnp.dot(x.astype(bf16), bd(y), preferred_element_type=f32)

    def precompute(g, carry):
        chains = []
        for i in range(unroll):
            c = g * unroll + i
            r0 = pl.multiple_of(c * C, C)
            cg = cg_ref[0, pl.ds(r0, C), :]
            for pp in range(npair):
                sl = slice(pp * LANES, (pp + 1) * LANES)
                q = q_ref[0, pl.ds(r0, C), sl]
                k = k_ref[0, pl.ds(r0, C), sl]
                v = v_ref[0, pl.ds(r0, C), sl]
                kbd = bd(k)
                for d in range(N_DIR):
                    def colb(grp, d=d, pp=pp, cg=cg):
                        i = grp * GATE_GROUP + (d * npair + pp) * 2
                        return jnp.where(la, cg[:, i:i + 1], cg[:, i + 1:i + 2])
                    chains.append(dict(c=c, r0=r0, d=d, pp=pp, q=q, k=k, v=v, kbd=kbd, beta=colb(0), gcol=colb(1),
                                       egc=colb(2), etg=colb(3)))
        for s in chains:
            incl = (row >= col) if s["d"] == 0 else (row <= col)
            grow = jnp.sum(jnp.where(diag, s["gcol"], 0.0), axis=0, keepdims=True)
            s["decay"] = jnp.where(incl, jnp.exp(jnp.where(incl, s["gcol"] - grow, 0.0)), 0.0)
            s["kb"] = s["k"] * s["beta"]
        for s in chains:
            s["kkqk"] = lax.dot_general(jnp.concatenate([s["kb"], s["q"]], axis=0).astype(bf16), s["kbd"], _NT,
                                        preferred_element_type=f32)
        for s in chains:
            incl = (row >= col) if s["d"] == 0 else (row <= col)
            strict = (row > col) if s["d"] == 0 else (row < col)
            lm = jnp.where(strict, s["kkqk"][:C] * s["decay"], 0.0)
            s["qkm"] = jnp.where(incl, s["kkqk"][C:] * s["decay"], 0.0)
            s["dg"] = jnp.where(same16, lm, 0.0)
            s["e"] = lm - s["dg"]
        for s in chains:
            s["d2"] = mm(s["dg"], s["dg"])
        for s in chains:
            s["d4"] = mm(s["d2"], s["d2"])
            s["dd2"] = mm(s["dg"], s["d2"])
        for s in chains:
            s["d8"] = mm(s["d4"], s["d4"])
        for s in chains:
            s["d48"] = mm(s["d4"], s["d8"])
        for s in chains:
            x1 = eye - s["dg"] + s["d2"] - s["dd2"]
            x2 = eye + s["d4"] + s["d8"] + s["d48"]
            s["t16"] = mm(x1, x2)
        for s in chains:
            s["n1"] = mm(s["t16"], s["e"])
        for s in chains:
            s["n2"] = mm(s["n1"], s["n1"])
        for s in chains:
            s["n3"] = mm(s["n1"], s["n2"])
        for s in chains:
            s["tinv"] = mm(eye - s["n1"] + s["n2"] - s["n3"], s["t16"])
        for s in chains:
            rhs = jnp.concatenate([s["v"] * s["beta"], s["kb"] * s["egc"]], axis=1)
            s["uw"] = mm(s["tinv"], rhs)
        for s in chains:
            s["nm"] = lax.dot_general(s["uw"].astype(bf16), (s["k"] * s["etg"]).astype(bf16), _TN,
                                      preferred_element_type=f32)
            s["oq"] = mm(s["qkm"], s["uw"])
        for s in chains:
            d, pp, r0 = s["d"], s["pp"], s["r0"]
            r2c = pl.multiple_of(s["c"] * 2 * C, 2 * C)
            last = C - 1 if d == 0 else 0
            nn_s[d, pp, pl.ds(r2c, 2 * C), :] = jnp.where(bd_mask, s["nm"][:LANES], 0.0)
            mt_s[d, pp, pl.ds(r2c, 2 * C), :] = jnp.where(bd_mask, s["nm"][LANES:], 0.0).astype(bf16)
            o_s[d, pp, pl.ds(r0, C), :] = s["oq"][:, :LANES]
            qt_s[d, pp, pl.ds(r0, C), :] = (s["q"] * s["egc"] - s["oq"][:, LANES:]).astype(bf16)
            et_s[d, pp, pl.ds(s["c"], 1), :] = s["egc"][last:last + 1, :]
        return carry

    lax.fori_loop(0, n_chunks // unroll, precompute, 0)

    idx = [(d, pp) for d in range(N_DIR) for pp in range(npair)]

    def scan(n, states):
        cs = [n, jnp.where(n < n_ctx, n_ctx - 1 - n, n_chunks + n_ctx - 1 - n)]
        r1 = [pl.multiple_of(c * C, C) for c in cs]
        r2c = [pl.multiple_of(c * 2 * C, 2 * C) for c in cs]
        stb = [st.astype(bf16) for st in states]
        prod = [jnp.dot(stb[j], mt_s[d, pp, pl.ds(r2c[d], 2 * C), :], preferred_element_type=f32)
                for j, (d, pp) in enumerate(idx)]
        new = tuple(states[j] * et_s[d, pp, pl.ds(cs[d], 1), :] - prod[j] + nn_s[d, pp, pl.ds(r2c[d], 2 * C), :]
                    for j, (d, pp) in enumerate(idx))
        for j, (d, pp) in enumerate(idx):
            o_s[d, pp, pl.ds(r1[d], C), :] = o_s[d, pp, pl.ds(r1[d], C), :] + lax.dot_general(
                qt_s[d, pp, pl.ds(r1[d], C), :], stb[j], _NT, preferred_element_type=f32)
        return new

    zero = jnp.zeros((2 * C, LANES), f32)
    lax.fori_loop(0, n_chunks, scan, tuple(zero for _ in idx))
    for pp in range(npair):
        o_ref[0, :, pp * LANES:(pp + 1) * LANES] = o_s[0, pp, n_ctx * C:, :] + o_s[1, pp, n_ctx * C:, :]


def _dn_delta(qkv, cg, n_ctx, unroll):
    B, L, _ = qkv.shape
    npair = PAIRS_PER_STEP
    ll = L - n_ctx * CHUNK
    nb = (HEADS * DK) // (npair * LANES)
    n_chunks = L // CHUNK
    w = npair * LANES
    return pl.pallas_call(
        functools.partial(_dn_kernel, n_ctx=n_ctx, unroll=unroll),
        grid=(B, PAIRS // npair),
        in_specs=[pl.BlockSpec((1, L, w), lambda b, p: (b, 0, p)),
                  pl.BlockSpec((1, L, w), lambda b, p: (b, 0, nb + p)),
                  pl.BlockSpec((1, L, w), lambda b, p: (b, 0, 2 * nb + p)),
                  pl.BlockSpec((1, L, LANES), lambda b, p: (b, 0, p))],
        out_specs=pl.BlockSpec((1, ll, w), lambda b, p: (b, 0, p)),
        out_shape=jax.ShapeDtypeStruct((B, ll, HEADS * DK), f32),
        scratch_shapes=[pltpu.VMEM((N_DIR, npair, 2 * L, LANES), bf16),
                        pltpu.VMEM((N_DIR, npair, 2 * L, LANES), f32),
                        pltpu.VMEM((N_DIR, npair, L, LANES), bf16),
                        pltpu.VMEM((N_DIR, npair, n_chunks, LANES), f32),
                        pltpu.VMEM((N_DIR, npair, L, LANES), f32)],
        compiler_params=_cparams(("arbitrary", "arbitrary")),
        name="dn_delta",
    )(qkv, qkv, qkv, cg)


def _merge_kernel(x_ref, mod_ref, om_ref, zm_ref, od_ref, zd_ref, g_ref, wmo_ref, dnw_ref, wdo_ref, wout_ref, o_ref):
    ym = _dot(om_ref[0].astype(f32) * _silu(zm_ref[0].astype(f32)), wmo_ref[...])
    od = od_ref[0]
    lane = lax.broadcasted_iota(jnp.int32, (od.shape[0], LANES), 1)
    lo = lane < DK
    parts = []
    for j in range(od.shape[1] // LANES):
        y = od[:, j * LANES:(j + 1) * LANES]
        y2 = y * y
        s_lo = jnp.sum(jnp.where(lo, y2, 0.0), axis=-1, keepdims=True)
        s_hi = jnp.sum(jnp.where(lo, 0.0, y2), axis=-1, keepdims=True)
        parts.append(y * lax.rsqrt(jnp.where(lo, s_lo, s_hi) * (1.0 / DK) + EPS))
    odn = jnp.concatenate(parts, axis=1) * dnw_ref[...]
    yd = _dot(odn * _silu(zd_ref[0].astype(f32)), wdo_ref[...])
    D = ym.shape[1]
    g = g_ref[0].astype(f32)
    y = _dot(jax.nn.sigmoid(g[:, :D]) * ym + jax.nn.sigmoid(g[:, D:]) * yd, wout_ref[...])
    o_ref[0] = x_ref[0] + mod_ref[0, 2:3, :] * y


def _merge(x, mod, o_mla, zm, o_dn, zd, gates, w_mo, dn_nw, w_do, w_out, tm):
    B, T, D = x.shape
    W = o_mla.shape[-1]
    tok = lambda n: pl.BlockSpec((1, tm, n), lambda b, t: (b, t, 0))
    return pl.pallas_call(
        _merge_kernel,
        grid=(B, T // tm),
        in_specs=[tok(D), pl.BlockSpec((1, 8, D), lambda b, t: (b, 0, 0)),
                  tok(W), tok(W), tok(W), tok(W), tok(2 * D),
                  _const_spec(w_mo.shape), _const_spec(dn_nw.shape), _const_spec(w_do.shape),
                  _const_spec(w_out.shape)],
        out_specs=tok(D),
        out_shape=jax.ShapeDtypeStruct((B, T, D), f32),
        compiler_params=_cparams(("arbitrary", "arbitrary")),
        name="merge",
    )(x, mod, o_mla, zm, o_dn, zd, gates, w_mo, dn_nw, w_do, w_out)


def _rope_partner():
    half = ROPE // 4
    p = np.arange(LANES)
    r = np.arange(ROPE)
    p[NOPE:QK] = NOPE + (r // (2 * half)) * 2 * half + (1 - (r % (2 * half)) // half) * half + r % half
    return p


def _rope_tables(T):
    half = ROPE // 4
    axis_dims = ROPE // 2
    pos = np.arange(T)
    inv_freq = (np.float32(ROPE_THETA) ** (-np.arange(0, axis_dims, 2, dtype=np.float32) / axis_dims)).astype(np.float32)
    ang = np.concatenate([(pos // GRID_W).astype(np.float32)[:, None] * inv_freq,
                          (pos % GRID_W).astype(np.float32)[:, None] * inv_freq], axis=1)
    cos32 = np.repeat(np.cos(ang).reshape(T, 2, 1, half), 2, axis=2).reshape(T, ROPE)
    sin32 = np.repeat(np.sin(ang).reshape(T, 2, 1, half), 2, axis=2).reshape(T, ROPE)
    sign = np.tile(np.concatenate([-np.ones(half), np.ones(half)]), 2)
    cos = np.concatenate([np.ones((T, NOPE)), cos32, np.zeros((T, LANES - QK))], axis=1)
    sin = np.concatenate([np.zeros((T, NOPE)), sin32 * sign, np.zeros((T, LANES - QK))], axis=1)
    return jnp.asarray(cos, f32), jnp.asarray(sin, f32)


def _pad_heads(w, per_head, keep):
    K = w.shape[0]
    wh = w.reshape(K, HEADS, per_head)[:, :, :keep]
    return jnp.pad(wh, ((0, 0), (0, 0), (0, LANES - keep)))


def _gate_lane_map():
    n = (PAIRS // PAIRS_PER_STEP) * LANES
    used = np.zeros(n, bool)
    grp = np.zeros(n, np.int64)
    dh = np.zeros(n, np.int64)
    for slot in range(PAIRS // PAIRS_PER_STEP):
        for g in range(4):
            for d in range(N_DIR):
                for pp in range(PAIRS_PER_STEP):
                    for a in range(2):
                        lane = slot * LANES + g * GATE_GROUP + (d * PAIRS_PER_STEP + pp) * 2 + a
                        used[lane], grp[lane] = True, g
                        dh[lane] = d * HEADS + (slot * PAIRS_PER_STEP + pp) * 2 + a
    return used, grp, dh


def kernel(x, c, ctx, c_ctx, w_mod, b_mod, norm_w, w_in, mla_q_norm_w, mla_w_uq, mla_kv_norm_w, mla_w_ukv,
           mla_q_head_norm_w, mla_k_head_norm_w, mla_w_o, dn_conv_w, dn_a_log, dn_dt_bias, dn_out_norm_w, dn_w_o,
           w_out):
    B, T, D = x.shape
    LC = ctx.shape[1]
    assert w_mod.shape[0] == 1, "one layer"
    assert GATE_GROUP == N_DIR * PAIRS_PER_STEP * 2 and 4 * GATE_GROUP <= LANES
    (w_mod, b_mod, norm_w, w_in, mla_q_norm_w, mla_w_uq, mla_kv_norm_w, mla_w_ukv, mla_q_head_norm_w,
     mla_k_head_norm_w, mla_w_o, dn_conv_w, dn_a_log, dn_dt_bias, dn_out_norm_w, dn_w_o, w_out) = (
        a[0] for a in (w_mod, b_mod, norm_w, w_in, mla_q_norm_w, mla_w_uq, mla_kv_norm_w, mla_w_ukv,
                       mla_q_head_norm_w, mla_k_head_norm_w, mla_w_o, dn_conv_w, dn_a_log, dn_dt_bias,
                       dn_out_norm_w, dn_w_o, w_out))
    QL, KVL = mla_w_uq.shape[0], mla_w_ukv.shape[0]
    WM, WK = HEADS * VD, HEADS * DK
    nh = N_DIR * HEADS

    R = -(-(B + 1) // 8) * 8
    cc = jnp.concatenate([c, c_ctx[None], jnp.zeros((R - B - 1, D), f32)], axis=0)
    mod = _modulation(cc, w_mod, b_mod).reshape(R, 3, D)
    mod8 = jnp.pad(mod, ((0, 0), (0, 5), (0, 0)))
    mod_lat = mod8[:B]
    mod_ctx = jnp.broadcast_to(mod8[B:B + 1], (B, 8, D))

    o = np.cumsum([0, QL, KVL, ROPE, WM, WK, WK, WK, WK, nh, nh, 2 * D])
    wcol = lambda i: w_in[:, o[i]:o[i + 1]]
    zeros = lambda n: jnp.zeros((D, n), f32)
    used, grp, dh = _gate_lane_map()
    gate_cols = np.where(grp == 0, o[8] + dh, o[9] + dh)
    w_g = jnp.where(jnp.asarray(used)[None, :], jnp.take(w_in, jnp.asarray(gate_cols), axis=1), 0.0)
    w_kr = jnp.concatenate([zeros(NOPE), wcol(2), zeros(LANES - QK)], axis=1)
    w_qkv = jnp.concatenate([wcol(4), wcol(5), wcol(6)], axis=1)
    w_all = jnp.concatenate([wcol(0), wcol(1), w_kr, wcol(3), w_qkv, wcol(7), w_g, wcol(10)], axis=1).astype(bf16)
    widths = [QL, KVL, LANES, WM, 3 * WK, WK, w_g.shape[1], 2 * D]
    dtypes = [bf16, bf16, f32, bf16, f32, bf16, f32, bf16]

    partner = _rope_partner()
    is_rope = jnp.asarray((np.arange(LANES) >= NOPE) & (np.arange(LANES) < QK))
    wq3 = _pad_heads(mla_w_uq, QK, QK)
    wq = jnp.concatenate([wq3.reshape(QL, HEADS * LANES),
                          jnp.where(is_rope, wq3[:, :, partner], 0.0).reshape(QL, HEADS * LANES)], axis=1).astype(bf16)
    wk = _pad_heads(mla_w_ukv, NOPE + VD, NOPE).reshape(KVL, HEADS * LANES).astype(bf16)
    wv = mla_w_ukv.reshape(KVL, HEADS, NOPE + VD)[:, :, NOPE:].reshape(KVL, WM).astype(bf16)
    slot_w = lambda hw: jnp.stack([jnp.pad(hw, (0, LANES - QK)),
                                   jnp.where(is_rope, jnp.pad(hw, (0, LANES - QK))[partner], 0.0)])
    kv_args = (mla_kv_norm_w.reshape(1, KVL), wk, wv, slot_w(mla_k_head_norm_w))
    q_args = (mla_q_norm_w.reshape(1, QL), wq, slot_w(mla_q_head_norm_w))
    q_scale = (QK ** -0.5) * math.log2(math.e)
    q, k_lat, v_lat, zm, qkv_l, zd, g_l, gates = _project(
        x, mod_lat, norm_w, w_all, widths, dtypes, (3, 4, 5, 6, 7), kv_args, q_args, _rope_tables(T), q_scale, 512)
    k_ctx, v_ctx, qkv_c, g_c = _project(ctx, mod_ctx, norm_w, w_all, widths, dtypes, (4, 6), kv_args, None, None,
                                        q_scale, LC)

    o_mla = _attention(q, k_ctx, k_lat, v_ctx, v_lat, 1024, 512, 512)

    conv_w8 = jnp.pad(dn_conv_w, ((0, 8 - CONV_W), (0, 0)))
    qkv = _dn_conv(qkv_c, qkv_l, conv_w8, 256)
    decay_lane = jnp.asarray(used & (grp > 0))
    lane_par = lambda p: jnp.where(decay_lane, p.reshape(nh)[jnp.asarray(dh)], 0.0).reshape(1, -1)
    cg = _dn_gates(g_c, g_l, lane_par(dn_a_log), lane_par(dn_dt_bias))
    o_dn = _dn_delta(qkv, cg, LC // CHUNK, 6)

    dn_nw = jnp.tile(dn_out_norm_w, HEADS).reshape(1, WK)
    return _merge(x, mod_lat, o_mla, zm, o_dn, zd, gates, mla_w_o.astype(bf16), dn_nw, dn_w_o.astype(bf16),
                  w_out.astype(bf16), 512)
```

```python
import functools
import math

import numpy as np
import jax
import jax.numpy as jnp
from jax import lax
from jax.experimental import pallas as pl
from jax.experimental.pallas import tpu as pltpu

f32 = jnp.float32
bf16 = jnp.bfloat16

HEADS = 8
NOPE = 64
ROPE = 32
QK = NOPE + ROPE
VD = 64
DK = 64
N_DIR = 2
CONV_W = 5
CHUNK = 64
GRID_W = 64
ROPE_THETA = 10000.0
EPS = 1e-6

LANES = 128
VMEM_LIMIT = 56 * 1024 * 1024
PAIRS = HEADS // 2
PAIRS_PER_STEP = 2
GATE_GROUP = 8

_NT = (((1,), (1,)), ((), ()))
_TN = (((0,), (0,)), ((), ()))


def _cparams(sem):
    return pltpu.CompilerParams(dimension_semantics=sem, vmem_limit_bytes=VMEM_LIMIT)


def _dot(a, b):
    return jnp.dot(a.astype(bf16), b.astype(bf16), preferred_element_type=f32)


def _silu(x):
    return x * jax.nn.sigmoid(x)


def _rms(x):
    return x * lax.rsqrt(jnp.mean(x * x, axis=-1, keepdims=True) + EPS)


def _const_spec(shape):
    nd = len(shape)
    return pl.BlockSpec(shape, lambda *_: (0,) * nd, pipeline_mode=pl.Buffered(1))


def _mod_kernel(c_ref, w_ref, b_ref, o_ref):
    o_ref[...] = jnp.dot(_silu(c_ref[...]), w_ref[...], preferred_element_type=f32,
                         precision=lax.Precision.HIGHEST) + b_ref[...]


def _modulation(cc, w_mod, b_mod):
    R, D = cc.shape
    N = w_mod.shape[1]
    nb = N // D
    return pl.pallas_call(
        _mod_kernel,
        grid=(nb,),
        in_specs=[pl.BlockSpec((R, D), lambda j: (0, 0)),
                  pl.BlockSpec((D, D), lambda j: (0, j)),
                  pl.BlockSpec((1, D), lambda j: (0, j))],
        out_specs=pl.BlockSpec((R, D), lambda j: (0, j)),
        out_shape=jax.ShapeDtypeStruct((R, N), f32),
        compiler_params=_cparams(("arbitrary",)),
        name="modulation",
    )(cc, w_mod, b_mod.reshape(1, N))


def _gate_split(x, al, dt):
    z = x + dt
    g = -jnp.exp(al) * (jnp.maximum(z, 0.0) + jnp.log1p(jnp.exp(-jnp.abs(z))))
    hi = g.astype(bf16)
    r1 = g - hi.astype(f32)
    mid = r1.astype(bf16)
    return g, (hi, mid, (r1 - mid.astype(f32)).astype(bf16))


def _gate_finish(x, g, terms, tri_ref):
    L, W = x.shape
    tri_sum = lambda m: sum(jnp.dot(m, term, preferred_element_type=f32) for term in terms)
    pre = tri_sum(tri_ref[0, :L, :L])
    suf = tri_sum(tri_ref[1, :L, :L])
    lane = lax.broadcasted_iota(jnp.int32, (L, W), 1) % LANES
    fwd = (lane % GATE_GROUP) < GATE_GROUP // N_DIR
    gcum = jnp.where(fwd, pre, suf)
    tot = pre + suf - g
    grp = lane // GATE_GROUP
    return jnp.where(grp == 0, jax.nn.sigmoid(x),
                     jnp.where(grp == 1, gcum, jnp.where(grp == 2, jnp.exp(gcum), jnp.exp(tot - gcum))))


def _conv_taps(pad_ref, cw, j, tm, tile):
    half = CONV_W // 2
    sl = slice(j * LANES, (j + 1) * LANES)
    ys = []
    for t0 in range(0, tm, tile):
        y = None
        for tap in range(CONV_W):
            r = 8 + t0 - half + tap
            term = pad_ref[r:r + tile, sl] * cw[tap:tap + 1, sl]
            y = term if y is None else y + term
        ys.append(_silu(y))
    return ys


def _conv_store(ys, o_ref, j, tile, scale):
    sl = slice(j * LANES, (j + 1) * LANES)
    if scale is not None:
        head_ones = jnp.where((lax.broadcasted_iota(jnp.int32, (LANES, LANES), 0) // DK)
                              == (lax.broadcasted_iota(jnp.int32, (LANES, LANES), 1) // DK), 1.0, 0.0).astype(bf16)
    for i, y in enumerate(ys):
        if scale is not None:
            ss = jnp.dot((y * y).astype(bf16), head_ones, preferred_element_type=f32)
            y = y * (lax.rsqrt(ss + EPS) * scale)
        o_ref[0, i * tile:(i + 1) * tile, sl] = y


def _proj_kernel(*refs, lead, segs, seg_qkv, seg_g, with_q, rope, q_scale, conv_tile, mxu_cols):
    it = iter(refs)
    (x_ref, xp_ref, xn_ref, mod_ref, nw_ref, w_ref, kvnw_ref, wk_ref, wv_ref, hwk_ref, cw_ref, al_ref, dt_ref,
     tri_ref) = (next(it) for _ in range(14))
    if with_q:
        qnw_ref, wq_ref, hwq_ref = (next(it) for _ in range(3))
    if rope:
        cos_ref, sin_ref = next(it), next(it)
    if with_q:
        q_ref = next(it)
    k_ref, v_ref = next(it), next(it)
    *out_refs, qkv_ref, cg_ref, pad_ref = list(it)

    nk = HEADS * LANES
    tm = x_ref.shape[1]
    norm_mod = lambda xx: (_rms(xx) * nw_ref[...] * (1.0 + mod_ref[0, 1:2, :]) + mod_ref[0, 0:1, :]).astype(bf16)
    h = norm_mod(x_ref[0])
    part = lambda a, b: jnp.dot(h, w_ref[:, a:b], preferred_element_type=f32)

    if with_q:
        cqn = (_rms(part(*lead[0])) * qnw_ref[...]).astype(bf16)
    ckvn = (_rms(part(*lead[1])) * kvnw_ref[...]).astype(bf16)
    kr = part(*lead[2])
    if with_q:
        qq = jnp.dot(cqn, wq_ref[...], preferred_element_type=f32)
    ka = jnp.dot(ckvn, wk_ref[...], preferred_element_type=f32)
    v_ref[0] = jnp.dot(ckvn, wv_ref[...], preferred_element_type=f32).astype(v_ref.dtype)

    if rope:
        cos, sin = cos_ref[...], sin_ref[...]
        lane = lax.broadcasted_iota(jnp.int32, kr.shape, 1)
        krs = jnp.where((lane & 8) != 0, pltpu.roll(kr, 8, 1), pltpu.roll(kr, LANES - 8, 1))
        ka_c, ka_s = hwk_ref[0:1, :] * cos, hwk_ref[1:2, :] * sin
        if with_q:
            qa_c, qa_s = (hwq_ref[0:1, :] * q_scale) * cos, (hwq_ref[1:2, :] * q_scale) * sin

    def head(hd):
        sl = slice(hd * LANES, (hd + 1) * LANES)
        xk = ka[:, sl] + kr
        rk = lax.rsqrt(jnp.sum(xk * xk, axis=-1, keepdims=True) * (1.0 / QK) + EPS)
        if rope:
            k_ref[0, :, sl] = ((xk * ka_c + krs * ka_s) * rk).astype(k_ref.dtype)
        else:
            k_ref[0, :, sl] = (xk * hwk_ref[0:1, :] * rk).astype(k_ref.dtype)
        if with_q:
            xq = qq[:, sl]
            rq = lax.rsqrt(jnp.sum(xq * xq, axis=-1, keepdims=True) * (1.0 / QK) + EPS)
            q_ref[0, :, sl] = ((xq * qa_c + qq[:, nk + hd * LANES:nk + (hd + 1) * LANES] * qa_s) * rq
                               ).astype(q_ref.dtype)

    def interleave(mxu_jobs, vec_jobs):
        per = -(-len(vec_jobs) // max(len(mxu_jobs), 1))
        for job in mxu_jobs:
            job()
            for _ in range(min(per, len(vec_jobs))):
                vec_jobs.pop(0)()
        while vec_jobs:
            vec_jobs.pop(0)()

    t = pl.program_id(1)
    q0, q1 = seg_qkv
    halo = jnp.dot(norm_mod(jnp.concatenate([xp_ref[0], xn_ref[0]], axis=0)), w_ref[:, q0:q1],
                   preferred_element_type=f32)
    pad_ref[0:8, :] = halo[:8] * jnp.where(t > 0, 1.0, 0.0)
    pad_ref[8 + tm:16 + tm, :] = halo[8:] * jnp.where(t < pl.num_programs(1) - 1, 1.0, 0.0)

    def qkv_cols(a0, b0):
        pad_ref[8:8 + tm, a0 - q0:b0 - q0] = part(a0, b0)

    interleave([functools.partial(qkv_cols, a0, min(a0 + mxu_cols, q1)) for a0 in range(q0, q1, mxu_cols)],
               [functools.partial(head, hd) for hd in range(HEADS)])
    gx = part(*seg_g)

    def seg_cols(o_ref, a, a0, b0):
        o_ref[0, :, a0 - a:b0 - a] = part(a0, b0).astype(o_ref.dtype)

    cw = cw_ref[...]
    nq = (HEADS * DK) // LANES
    held = {}

    def stage1(j):
        if j == "g":
            held[j] = _gate_split(gx, al_ref[...], dt_ref[...])
        else:
            held[j] = _conv_taps(pad_ref, cw, j, tm, conv_tile)

    def stage2(j):
        if j == "g":
            g, terms = held.pop(j)
            cg_ref[0] = _gate_finish(gx, g, terms, tri_ref)
        else:
            scale = DK ** -0.5 if j < nq else (1.0 if j < 2 * nq else None)
            _conv_store(held.pop(j), qkv_ref, j, conv_tile, scale)

    order = ["g"] + list(range((q1 - q0) // LANES))
    vec_jobs = [functools.partial(stage1, order[0])]
    for prev, cur in zip(order, order[1:]):
        vec_jobs.append(lambda prev=prev, cur=cur: (stage1(cur), stage2(prev)))
    vec_jobs.append(functools.partial(stage2, order[-1]))
    interleave([functools.partial(seg_cols, o_ref, a, a0, min(a0 + mxu_cols, b)) for (a, b), o_ref in zip(segs, out_refs)
                for a0 in range(a, b, mxu_cols)], vec_jobs)


def _project(x, mod, norm_w, w, widths, dtypes, use, i_qkv, i_g, kv, dn, q, tabs, q_scale, tm, conv_tile):
    B, L, D = x.shape
    offs = np.concatenate([[0], np.cumsum(widths)])
    rng = lambda i: (int(offs[i]), int(offs[i + 1]))
    tok = lambda n: pl.BlockSpec((1, tm, n), lambda b, t: (b, t, 0))
    r8 = tm // 8
    args = [x, x, x, mod, norm_w.reshape(1, D), w, *kv, *dn]
    in_specs = [tok(D),
                pl.BlockSpec((1, 8, D), lambda b, t: (b, jnp.maximum(t * r8 - 1, 0), 0)),
                pl.BlockSpec((1, 8, D), lambda b, t: (b, jnp.minimum((t + 1) * r8, L // 8 - 1), 0)),
                pl.BlockSpec((1, 8, D), lambda b, t: (b, 0, 0)), _const_spec((1, D)), _const_spec(w.shape)]
    in_specs += [_const_spec(a.shape) for a in (*kv, *dn)]
    if q is not None:
        args += list(q)
        in_specs += [_const_spec(a.shape) for a in q]
    if tabs is not None:
        args += list(tabs)
        in_specs += [pl.BlockSpec((tm, LANES), lambda b, t: (t, 0))] * 2
    nk = HEADS * LANES
    out_widths = (([nk] if q is not None else []) + [nk, HEADS * VD] + [widths[i] for i in use]
                  + [widths[i_qkv], widths[i_g]])
    out_dtypes = ([bf16] if q is not None else []) + [bf16, bf16] + [dtypes[i] for i in use] + [f32, f32]
    return pl.pallas_call(
        functools.partial(_proj_kernel, lead=(rng(0), rng(1), rng(2)), segs=tuple(rng(i) for i in use),
                          seg_qkv=rng(i_qkv), seg_g=rng(i_g), with_q=q is not None, rope=tabs is not None,
                          q_scale=q_scale, conv_tile=conv_tile, mxu_cols=512),
        grid=(B, L // tm),
        in_specs=in_specs,
        out_specs=[tok(n) for n in out_widths],
        out_shape=[jax.ShapeDtypeStruct((B, L, n), dt) for n, dt in zip(out_widths, out_dtypes)],
        scratch_shapes=[pltpu.VMEM((tm + 16, widths[i_qkv]), f32)],
        compiler_params=_cparams(("arbitrary", "arbitrary")),
        name="project_lat" if q is not None else "project_ctx",
    )(*args)


def _attn_kernel(q_ref, kc_ref, kl_ref, vc_ref, vl_ref, o_ref, s_ref, *, kv_chunk, rows):
    lc = kc_ref.shape[1]
    ll = kl_ref.shape[1]
    chunks = [(kc_ref, vc_ref, 0, lc, 0)]
    for off in range(0, ll, kv_chunk):
        chunks.append((kl_ref, vl_ref, off, kv_chunk, lc + off))
    nck = len(chunks)
    streams = [(r0, a) for r0 in range(0, q_ref.shape[1], rows) for a in range(2)]

    def scores(j, i, m):
        r0, a = streams[j]
        k_ref, _, off, n, col = chunks[i]
        sl = slice(a * LANES, (a + 1) * LANES)
        s = lax.dot_general(q_ref[0, r0:r0 + rows, sl], k_ref[0, off:off + n, sl], _NT, preferred_element_type=f32)
        s_ref[j % 2, :, col:col + n] = s
        cm = jnp.max(s, axis=-1, keepdims=True)
        return cm if m is None else jnp.maximum(m, cm)

    def weighted(j, i, m, l, acc):
        _, v_ref, off, n, col = chunks[i]
        p = jnp.exp2(s_ref[j % 2, :, col:col + n] - m)
        ps = jnp.sum(p, axis=-1, keepdims=True)
        pv = jnp.dot(p.astype(bf16), v_ref[0, off:off + n, :], preferred_element_type=f32)
        return (ps if l is None else l + ps), (pv if acc is None else acc + pv)

    ns = len(streams)
    m = [None] * ns
    res = [None] * ns
    for j in range(ns + 1):
        l, acc = None, None
        for i in range(nck):
            if j > 0:
                l, acc = weighted(j - 1, i, m[j - 1], l, acc)
            if j < ns:
                m[j] = scores(j, i, m[j])
        if j > 0:
            res[j - 1] = acc / l
    lane = lax.broadcasted_iota(jnp.int32, res[0].shape, 1)
    for j in range(0, ns, 2):
        r0 = streams[j][0]
        o_ref[0, r0:r0 + rows, :] = jnp.where(lane < VD, res[j], res[j + 1]).astype(o_ref.dtype)


def _attention(q, k_ctx, k_lat, v_ctx, v_lat, tq, rows, kv_chunk):
    B, T, _ = q.shape
    lc, ll = k_ctx.shape[1], k_lat.shape[1]
    return pl.pallas_call(
        functools.partial(_attn_kernel, kv_chunk=kv_chunk, rows=rows),
        grid=(B, PAIRS, T // tq),
        in_specs=[pl.BlockSpec((1, tq, 2 * LANES), lambda b, p, t: (b, t, p)),
                  pl.BlockSpec((1, lc, 2 * LANES), lambda b, p, t: (b, 0, p)),
                  pl.BlockSpec((1, ll, 2 * LANES), lambda b, p, t: (b, 0, p)),
                  pl.BlockSpec((1, lc, LANES), lambda b, p, t: (b, 0, p)),
                  pl.BlockSpec((1, ll, LANES), lambda b, p, t: (b, 0, p))],
        out_specs=pl.BlockSpec((1, tq, LANES), lambda b, p, t: (b, t, p)),
        out_shape=jax.ShapeDtypeStruct((B, T, HEADS * VD), bf16),
        scratch_shapes=[pltpu.VMEM((2, rows, lc + ll), f32)],
        compiler_params=_cparams(("arbitrary", "arbitrary", "arbitrary")),
        name="mla_attention",
    )(q, k_ctx, k_lat, v_ctx, v_lat)


def _dn_kernel(qc_ref, kc_ref, vc_ref, gc_ref, ql_ref, kl_ref, vl_ref, gl_ref, o_ref,
               q_ref, k_ref, v_ref, cg_ref, mt_s, nn_s, qt_s, et_s, o_s, *, unroll):
    n_ctx = qc_ref.shape[1] // CHUNK
    lc = qc_ref.shape[1]
    for src_c, src_l, dst in ((qc_ref, ql_ref, q_ref), (kc_ref, kl_ref, k_ref), (vc_ref, vl_ref, v_ref),
                              (gc_ref, gl_ref, cg_ref)):
        dst[0, 0:lc, :] = src_c[0]
        dst[0, lc:, :] = src_l[0]
    C = CHUNK
    n_chunks = q_ref.shape[1] // C
    npair = q_ref.shape[2] // LANES
    row = lax.broadcasted_iota(jnp.int32, (C, LANES), 0)
    lane = lax.broadcasted_iota(jnp.int32, (C, LANES), 1)
    col = lane % C
    la = lane < C
    same16 = (row // 16) == (col // 16)
    diag = row == col
    eye = jnp.where(diag, 1.0, 0.0).astype(f32)
    r2 = lax.broadcasted_iota(jnp.int32, (2 * C, LANES), 0)
    l2 = lax.broadcasted_iota(jnp.int32, (2 * C, LANES), 1)
    bd_mask = (r2 < C) == (l2 < C)
    rmask = jnp.concatenate([bd_mask, bd_mask], axis=1)

    def bd(y):
        m = bd_mask if y.shape[1] == LANES else rmask
        return jnp.where(m, jnp.concatenate([y, y], axis=0), 0.0).astype(bf16)

    def mm(x, y):
        return jnp.dot(x.astype(bf16), bd(y), preferred_element_type=f32)

    def precompute(g, carry):
        chains = []
        for i in range(unroll):
            c = g * unroll + i
            r0 = pl.multiple_of(c * C, C)
            cg = cg_ref[0, pl.ds(r0, C), :]
            for pp in range(npair):
                sl = slice(pp * LANES, (pp + 1) * LANES)
                q = q_ref[0, pl.ds(r0, C), sl]
                k = k_ref[0, pl.ds(r0, C), sl]
                v = v_ref[0, pl.ds(r0, C), sl]
                kbd = bd(k)
                for d in range(N_DIR):
                    def colb(grp, d=d, pp=pp, cg=cg):
                        i = grp * GATE_GROUP + (d * npair + pp) * 2
                        return jnp.where(la, cg[:, i:i + 1], cg[:, i + 1:i + 2])
                    chains.append(dict(c=c, r0=r0, d=d, pp=pp, q=q, k=k, v=v, kbd=kbd, beta=colb(0), gcol=colb(1),
                                       egc=colb(2), etg=colb(3)))
        for s in chains:
            incl = (row >= col) if s["d"] == 0 else (row <= col)
            grow = jnp.sum(jnp.where(diag, s["gcol"], 0.0), axis=0, keepdims=True)
            s["decay"] = jnp.where(incl, jnp.exp(jnp.where(incl, s["gcol"] - grow, 0.0)), 0.0)
            s["kb"] = s["k"] * s["beta"]
        for s in chains:
            s["kkqk"] = lax.dot_general(jnp.concatenate([s["kb"], s["q"]], axis=0).astype(bf16), s["kbd"], _NT,
                                        preferred_element_type=f32)
        for s in chains:
            incl = (row >= col) if s["d"] == 0 else (row <= col)
            strict = (row > col) if s["d"] == 0 else (row < col)
            lm = jnp.where(strict, s["kkqk"][:C] * s["decay"], 0.0)
            s["qkm"] = jnp.where(incl, s["kkqk"][C:] * s["decay"], 0.0)
            s["dg"] = jnp.where(same16, lm, 0.0)
            s["e"] = lm - s["dg"]
        for s in chains:
            s["d2"] = mm(s["dg"], s["dg"])
        for s in chains:
            p0 = eye - s["dg"]
            both = mm(jnp.concatenate([p0, s["d2"]], axis=0), s["d2"])
            s["p"], s["d4"] = p0 + both[:C], both[C:]
        for s in chains:
            both = mm(jnp.concatenate([s["p"], s["d4"]], axis=0), s["d4"])
            s["p"], s["d8"] = s["p"] + both[:C], both[C:]
        for s in chains:
            s["t16"] = s["p"] + mm(s["p"], s["d8"])
        for s in chains:
            s["n1"] = mm(s["t16"], s["e"])
        for s in chains:
            s["n2"] = mm(s["n1"], s["n1"])
        for s in chains:
            im = eye - s["n1"]
            s["qn"] = im + mm(im, s["n2"])
        for s in chains:
            s["tinv"] = mm(s["qn"], s["t16"])
        for s in chains:
            s["g"] = mm(s["qkm"], s["tinv"])
        for s in chains:
            rhs = jnp.concatenate([s["v"] * s["beta"], s["kb"] * s["egc"]], axis=1)
            both = mm(jnp.concatenate([s["tinv"], s["g"]], axis=0), rhs)
            s["uw"], s["oq"] = both[:C], both[C:]
        for s in chains:
            s["nm"] = lax.dot_general(s["uw"].astype(bf16), (s["k"] * s["etg"]).astype(bf16), _TN,
                                      preferred_element_type=f32)
        for s in chains:
            d, pp, r0 = s["d"], s["pp"], s["r0"]
            r2c = pl.multiple_of(s["c"] * 2 * C, 2 * C)
            last = C - 1 if d == 0 else 0
            nn_s[d, pp, pl.ds(r2c, 2 * C), :] = jnp.where(bd_mask, s["nm"][:LANES], 0.0)
            mt_s[d, pp, pl.ds(r2c, 2 * C), :] = jnp.where(bd_mask, s["nm"][LANES:], 0.0).astype(bf16)
            o_s[d, pp, pl.ds(r0, C), :] = s["oq"][:, :LANES]
            qt_s[d, pp, pl.ds(r0, C), :] = (s["q"] * s["egc"] - s["oq"][:, LANES:]).astype(bf16)
            et_s[d, pp, pl.ds(s["c"], 1), :] = s["egc"][last:last + 1, :]
        return carry

    lax.fori_loop(0, n_chunks // unroll, precompute, 0)

    idx = [(d, pp) for d in range(N_DIR) for pp in range(npair)]

    def scan(n, states):
        cs = [n, jnp.where(n < n_ctx, n_ctx - 1 - n, n_chunks + n_ctx - 1 - n)]
        r1 = [pl.multiple_of(c * C, C) for c in cs]
        r2c = [pl.multiple_of(c * 2 * C, 2 * C) for c in cs]
        stb = [st.astype(bf16) for st in states]
        prod = [jnp.dot(stb[j], mt_s[d, pp, pl.ds(r2c[d], 2 * C), :], preferred_element_type=f32)
                for j, (d, pp) in enumerate(idx)]
        new = tuple(states[j] * et_s[d, pp, pl.ds(cs[d], 1), :] - prod[j] + nn_s[d, pp, pl.ds(r2c[d], 2 * C), :]
                    for j, (d, pp) in enumerate(idx))
        for j, (d, pp) in enumerate(idx):
            o_s[d, pp, pl.ds(r1[d], C), :] = o_s[d, pp, pl.ds(r1[d], C), :] + lax.dot_general(
                qt_s[d, pp, pl.ds(r1[d], C), :], stb[j], _NT, preferred_element_type=f32)
        return new

    zero = jnp.zeros((2 * C, LANES), f32)
    lax.fori_loop(0, n_chunks, scan, tuple(zero for _ in idx))
    for pp in range(npair):
        o_ref[0, :, pp * LANES:(pp + 1) * LANES] = o_s[0, pp, n_ctx * C:, :] + o_s[1, pp, n_ctx * C:, :]


def _dn_delta(qkv_c, cg_c, qkv_l, cg_l, unroll):
    B, lc, _ = qkv_c.shape
    ll = qkv_l.shape[1]
    L = lc + ll
    npair = PAIRS_PER_STEP
    nb = (HEADS * DK) // (npair * LANES)
    n_chunks = L // CHUNK
    w = npair * LANES
    specs = lambda rows: [pl.BlockSpec((1, rows, w), lambda b, p: (b, 0, p)),
                          pl.BlockSpec((1, rows, w), lambda b, p: (b, 0, nb + p)),
                          pl.BlockSpec((1, rows, w), lambda b, p: (b, 0, 2 * nb + p)),
                          pl.BlockSpec((1, rows, LANES), lambda b, p: (b, 0, p))]
    return pl.pallas_call(
        functools.partial(_dn_kernel, unroll=unroll),
        grid=(B, PAIRS // npair),
        in_specs=specs(lc) + specs(ll),
        out_specs=pl.BlockSpec((1, ll, w), lambda b, p: (b, 0, p)),
        out_shape=jax.ShapeDtypeStruct((B, ll, HEADS * DK), f32),
        scratch_shapes=[pltpu.VMEM((1, L, w), f32), pltpu.VMEM((1, L, w), f32), pltpu.VMEM((1, L, w), f32),
                        pltpu.VMEM((1, L, LANES), f32),
                        pltpu.VMEM((N_DIR, npair, 2 * L, LANES), bf16),
                        pltpu.VMEM((N_DIR, npair, 2 * L, LANES), f32),
                        pltpu.VMEM((N_DIR, npair, L, LANES), bf16),
                        pltpu.VMEM((N_DIR, npair, n_chunks, LANES), f32),
                        pltpu.VMEM((N_DIR, npair, L, LANES), f32)],
        compiler_params=_cparams(("arbitrary", "arbitrary")),
        name="dn_delta",
    )(qkv_c, qkv_c, qkv_c, cg_c, qkv_l, qkv_l, qkv_l, cg_l)


def _merge_kernel(x_ref, mod_ref, om_ref, zm_ref, od_ref, zd_ref, g_ref, wmo_ref, dnw_ref, wdo_ref, wout_ref, o_ref):
    ym = _dot(om_ref[0].astype(f32) * _silu(zm_ref[0].astype(f32)), wmo_ref[...])
    od = od_ref[0]
    lane = lax.broadcasted_iota(jnp.int32, (od.shape[0], LANES), 1)
    lo = lane < DK
    parts = []
    for j in range(od.shape[1] // LANES):
        y = od[:, j * LANES:(j + 1) * LANES]
        y2 = y * y
        s_lo = jnp.sum(jnp.where(lo, y2, 0.0), axis=-1, keepdims=True)
        s_hi = jnp.sum(jnp.where(lo, 0.0, y2), axis=-1, keepdims=True)
        parts.append(y * lax.rsqrt(jnp.where(lo, s_lo, s_hi) * (1.0 / DK) + EPS))
    odn = jnp.concatenate(parts, axis=1) * dnw_ref[...]
    yd = _dot(odn * _silu(zd_ref[0].astype(f32)), wdo_ref[...])
    D = ym.shape[1]
    g = g_ref[0].astype(f32)
    y = _dot(jax.nn.sigmoid(g[:, :D]) * ym + jax.nn.sigmoid(g[:, D:]) * yd, wout_ref[...])
    o_ref[0] = x_ref[0] + mod_ref[0, 2:3, :] * y


def _merge(x, mod, o_mla, zm, o_dn, zd, gates, w_mo, dn_nw, w_do, w_out, tm):
    B, T, D = x.shape
    W = o_mla.shape[-1]
    tok = lambda n: pl.BlockSpec((1, tm, n), lambda b, t: (b, t, 0))
    return pl.pallas_call(
        _merge_kernel,
        grid=(B, T // tm),
        in_specs=[tok(D), pl.BlockSpec((1, 8, D), lambda b, t: (b, 0, 0)),
                  tok(W), tok(W), tok(W), tok(W), tok(2 * D),
                  _const_spec(w_mo.shape), _const_spec(dn_nw.shape), _const_spec(w_do.shape),
                  _const_spec(w_out.shape)],
        out_specs=tok(D),
        out_shape=jax.ShapeDtypeStruct((B, T, D), f32),
        compiler_params=_cparams(("arbitrary", "arbitrary")),
        name="merge",
    )(x, mod, o_mla, zm, o_dn, zd, gates, w_mo, dn_nw, w_do, w_out)


def _rope_partner():
    half = ROPE // 4
    p = np.arange(LANES)
    r = np.arange(ROPE)
    p[NOPE:QK] = NOPE + (r // (2 * half)) * 2 * half + (1 - (r % (2 * half)) // half) * half + r % half
    return p


def _rope_tables(T):
    half = ROPE // 4
    axis_dims = ROPE // 2
    pos = np.arange(T)
    inv_freq = (np.float32(ROPE_THETA) ** (-np.arange(0, axis_dims, 2, dtype=np.float32) / axis_dims)).astype(np.float32)
    ang = np.concatenate([(pos // GRID_W).astype(np.float32)[:, None] * inv_freq,
                          (pos % GRID_W).astype(np.float32)[:, None] * inv_freq], axis=1)
    cos32 = np.repeat(np.cos(ang).reshape(T, 2, 1, half), 2, axis=2).reshape(T, ROPE)
    sin32 = np.repeat(np.sin(ang).reshape(T, 2, 1, half), 2, axis=2).reshape(T, ROPE)
    sign = np.tile(np.concatenate([-np.ones(half), np.ones(half)]), 2)
    cos = np.concatenate([np.ones((T, NOPE)), cos32, np.zeros((T, LANES - QK))], axis=1)
    sin = np.concatenate([np.zeros((T, NOPE)), sin32 * sign, np.zeros((T, LANES - QK))], axis=1)
    return jnp.asarray(cos, f32), jnp.asarray(sin, f32)


def _pad_heads(w, per_head, keep):
    K = w.shape[0]
    wh = w.reshape(K, HEADS, per_head)[:, :, :keep]
    return jnp.pad(wh, ((0, 0), (0, 0), (0, LANES - keep)))


def _gate_slots(db, da):
    slots = PAIRS // PAIRS_PER_STEP
    lead = db.shape[:-1]
    per_slot = lambda t: jnp.moveaxis(t.reshape(lead + (N_DIR, slots, HEADS // slots)), -2, -3).reshape(
        lead + (slots, GATE_GROUP))
    db, da = per_slot(db), per_slot(da)
    pad = jnp.zeros(lead + (slots, LANES - 4 * GATE_GROUP), db.dtype)
    return jnp.concatenate([db, da, da, da, pad], axis=-1).reshape(lead + (slots * LANES,))


def kernel(x, c, ctx, c_ctx, w_mod, b_mod, norm_w, w_in, mla_q_norm_w, mla_w_uq, mla_kv_norm_w, mla_w_ukv,
           mla_q_head_norm_w, mla_k_head_norm_w, mla_w_o, dn_conv_w, dn_a_log, dn_dt_bias, dn_out_norm_w, dn_w_o,
           w_out):
    B, T, D = x.shape
    LC = ctx.shape[1]
    assert w_mod.shape[0] == 1, "one layer"
    assert GATE_GROUP == N_DIR * PAIRS_PER_STEP * 2 and 4 * GATE_GROUP <= LANES
    (w_mod, b_mod, norm_w, w_in, mla_q_norm_w, mla_w_uq, mla_kv_norm_w, mla_w_ukv, mla_q_head_norm_w,
     mla_k_head_norm_w, mla_w_o, dn_conv_w, dn_a_log, dn_dt_bias, dn_out_norm_w, dn_w_o, w_out) = (
        a[0] for a in (w_mod, b_mod, norm_w, w_in, mla_q_norm_w, mla_w_uq, mla_kv_norm_w, mla_w_ukv,
                       mla_q_head_norm_w, mla_k_head_norm_w, mla_w_o, dn_conv_w, dn_a_log, dn_dt_bias,
                       dn_out_norm_w, dn_w_o, w_out))
    QL, KVL = mla_w_uq.shape[0], mla_w_ukv.shape[0]
    WM, WK = HEADS * VD, HEADS * DK
    nh = N_DIR * HEADS

    R = -(-(B + 1) // 8) * 8
    cc = jnp.concatenate([c, c_ctx[None], jnp.zeros((R - B - 1, D), f32)], axis=0)
    mod = _modulation(cc, w_mod, b_mod).reshape(R, 3, D)
    mod8 = jnp.pad(mod, ((0, 0), (0, 5), (0, 0)))
    mod_lat = mod8[:B]
    mod_ctx = jnp.broadcast_to(mod8[B:B + 1], (B, 8, D))

    o = np.cumsum([0, QL, KVL, ROPE, WM, WK, WK, WK, WK, nh, nh, 2 * D])
    wcol = lambda i: w_in[:, o[i]:o[i + 1]]
    zeros = lambda n: jnp.zeros((D, n), f32)
    w_g = _gate_slots(wcol(8), wcol(9))
    w_kr = jnp.concatenate([zeros(NOPE), wcol(2), zeros(LANES - QK)], axis=1)
    w_qkv = jnp.concatenate([wcol(4), wcol(5), wcol(6)], axis=1)
    w_all = jnp.concatenate([wcol(0), wcol(1), w_kr, wcol(3), w_qkv, wcol(7), w_g, wcol(10)], axis=1).astype(bf16)
    widths = [QL, KVL, LANES, WM, 3 * WK, WK, w_g.shape[1], 2 * D]
    dtypes = [bf16, bf16, f32, bf16, f32, bf16, f32, bf16]

    partner = _rope_partner()
    is_rope = jnp.asarray((np.arange(LANES) >= NOPE) & (np.arange(LANES) < QK))
    wq3 = _pad_heads(mla_w_uq, QK, QK)
    wq = jnp.concatenate([wq3.reshape(QL, HEADS * LANES),
                          jnp.where(is_rope, wq3[:, :, partner], 0.0).reshape(QL, HEADS * LANES)], axis=1).astype(bf16)
    wk = _pad_heads(mla_w_ukv, NOPE + VD, NOPE).reshape(KVL, HEADS * LANES).astype(bf16)
    wv = mla_w_ukv.reshape(KVL, HEADS, NOPE + VD)[:, :, NOPE:].reshape(KVL, WM).astype(bf16)
    slot_w = lambda hw: jnp.stack([jnp.pad(hw, (0, LANES - QK)),
                                   jnp.where(is_rope, jnp.pad(hw, (0, LANES - QK))[partner], 0.0)])
    kv_args = (mla_kv_norm_w.reshape(1, KVL), wk, wv, slot_w(mla_k_head_norm_w))
    q_args = (mla_q_norm_w.reshape(1, QL), wq, slot_w(mla_q_head_norm_w))
    q_scale = (QK ** -0.5) * math.log2(math.e)
    lane_par = lambda p: _gate_slots(jnp.zeros((1, nh), f32), p.reshape(1, nh))
    tm = 512
    r_i, c_i = np.arange(tm)[:, None], np.arange(tm)[None, :]
    same_chunk = (r_i // CHUNK) == (c_i // CHUNK)
    tri = jnp.asarray(np.stack([same_chunk & (c_i <= r_i), same_chunk & (c_i >= r_i)]), bf16)
    dn_args = (jnp.pad(dn_conv_w, ((0, 8 - CONV_W), (0, 0))), lane_par(dn_a_log), lane_par(dn_dt_bias), tri)
    q, k_lat, v_lat, zm, zd, gates, qkv_l, cg_l = _project(
        x, mod_lat, norm_w, w_all, widths, dtypes, (3, 5, 7), 4, 6, kv_args, dn_args, q_args, _rope_tables(T),
        q_scale, tm, 256)
    k_ctx, v_ctx, qkv_c, cg_c = _project(ctx, mod_ctx, norm_w, w_all, widths, dtypes, (), 4, 6, kv_args, dn_args,
                                         None, None, q_scale, LC, 256)

    o_mla = _attention(q, k_ctx, k_lat, v_ctx, v_lat, 1024, 512, 512)
    o_dn = _dn_delta(qkv_c, cg_c, qkv_l, cg_l, 6)

    dn_nw = jnp.tile(dn_out_norm_w, HEADS).reshape(1, WK)
    return _merge(x, mod_lat, o_mla, zm, o_dn, zd, gates, mla_w_o.astype(bf16), dn_nw, dn_w_o.astype(bf16),
                  w_out.astype(bf16), 512)
```
